```python
import math
import jax, jax.numpy as jnp
from jax import lax
import numpy as np

D_MODEL = 1024
BATCH = 2
SEQ = 16384
DEPTH = 2
DEC_BATCH = 8
DEC_SEQ = 4096
PAST_LEN = 128

N_MIXERS = 2
N_FOURIER_LAYERS = (DEPTH + 1) // 2
N_MLA_LAYERS = DEPTH // 2
N_FOURIER_GROUPS = 4
FOURIER_GROUP_DIM = D_MODEL // N_FOURIER_GROUPS
N_HEADS = 8
QK_NOPE_DIM = 128
QK_ROPE_DIM = 64
V_HEAD_DIM = 128
Q_LORA_RANK = 256
KV_LORA_RANK = 256
ROPE_THETA = 10000.0
Q_BLOCK = 128
D_FF = ((8 * D_MODEL // 3 + 255) // 256) * 256
NORM_EPS = 1e-6

kernel_name = "fnet_mla_interleaved_encoder"


def rms_norm(x, g):
    xf = x.astype(jnp.float32)
    y = xf * lax.rsqrt(jnp.mean(xf * xf, axis=-1, keepdims=True) + NORM_EPS)
    return (y * g.astype(jnp.float32)).astype(x.dtype)


def rope_tables(seq_len, dtype):
    inv_freq = 1.0 / (ROPE_THETA ** (jnp.arange(0, QK_ROPE_DIM, 2, dtype=jnp.float32) / QK_ROPE_DIM))
    ang = jnp.arange(seq_len, dtype=jnp.float32)[:, None] * inv_freq[None, :]
    return jnp.cos(ang).astype(dtype), jnp.sin(ang).astype(dtype)


def apply_rope(x, cos, sin):
    half = x.shape[-1] // 2
    x1, x2 = x[..., :half], x[..., half:]
    return jnp.concatenate([x1 * cos - x2 * sin, x2 * cos + x1 * sin], axis=-1)


def mixer_fourier(h, w_out, b_out):
    B, S, D = h.shape
    hg = h.astype(jnp.float32).reshape(B, S, N_FOURIER_GROUPS, FOURIER_GROUP_DIM)
    mixed = jnp.real(jnp.fft.fft2(hg, axes=(1, 3))).reshape(B, S, D).astype(h.dtype)
    return mixed @ w_out + b_out


def blocked_attention(q_nope, q_rope, k_nope, k_rope, v):
    B, S, H, Dn = q_nope.shape
    nb = S // Q_BLOCK
    scale = 1.0 / math.sqrt(QK_NOPE_DIM + QK_ROPE_DIM)
    qn = q_nope.reshape(B, nb, Q_BLOCK, H, Dn).transpose(1, 0, 2, 3, 4)
    qr = q_rope.reshape(B, nb, Q_BLOCK, H, QK_ROPE_DIM).transpose(1, 0, 2, 3, 4)

    def one_block(args):
        qn_b, qr_b = args
        s = (jnp.einsum('bqhd,bkhd->bhqk', qn_b, k_nope, preferred_element_type=jnp.float32)
             + jnp.einsum('bqhr,bkr->bhqk', qr_b, k_rope, preferred_element_type=jnp.float32)) * scale
        p = jax.nn.softmax(s, axis=-1).astype(v.dtype)
        return jnp.einsum('bhqk,bkhd->bqhd', p, v)

    out = lax.map(one_block, (qn, qr))
    return out.transpose(1, 0, 2, 3, 4).reshape(B, S, H, V_HEAD_DIM)


def mixer_mla(h, w_down, q_norm, w_uq, kv_norm, w_ukv, w_o):
    B, S, _ = h.shape
    down = h @ w_down
    c_q = down[..., :Q_LORA_RANK]
    c_kv = down[..., Q_LORA_RANK:Q_LORA_RANK + KV_LORA_RANK]
    k_rope = down[..., Q_LORA_RANK + KV_LORA_RANK:]
    q = (rms_norm(c_q, q_norm) @ w_uq).reshape(B, S, N_HEADS, QK_NOPE_DIM + QK_ROPE_DIM)
    q_nope, q_rope = q[..., :QK_NOPE_DIM], q[..., QK_NOPE_DIM:]
    kv = (rms_norm(c_kv, kv_norm) @ w_ukv).reshape(B, S, N_HEADS, QK_NOPE_DIM + V_HEAD_DIM)
    k_nope, v = kv[..., :QK_NOPE_DIM], kv[..., QK_NOPE_DIM:]
    cos, sin = rope_tables(S, h.dtype)
    q_rope = apply_rope(q_rope, cos[:, None, :], sin[:, None, :])
    k_rope = apply_rope(k_rope, cos, sin)
    o = blocked_attention(q_nope, q_rope, k_nope, k_rope, v)
    return o.reshape(B, S, N_HEADS * V_HEAD_DIM) @ w_o


def swiglu_ffn(h, w_gate, w_up, w_down):
    return (jax.nn.silu(h @ w_gate) * (h @ w_up)) @ w_down


def trunk(x, norm_g, fnet_w_out, fnet_b_out, mla_w_down, mla_q_norm, mla_w_uq,
          mla_kv_norm, mla_w_ukv, mla_w_o, ffn_w_gate, ffn_w_up, ffn_w_down):
    for i in range(DEPTH):
        g = norm_g[i]
        j = i // N_MIXERS
        hn = rms_norm(x, g[0])
        if i % N_MIXERS == 0:
            m = mixer_fourier(hn, fnet_w_out[j], fnet_b_out[j])
        else:
            m = mixer_mla(hn, mla_w_down[j], mla_q_norm[j], mla_w_uq[j],
                          mla_kv_norm[j], mla_w_ukv[j], mla_w_o[j])
        x = x + rms_norm(m, g[1])
        f = swiglu_ffn(rms_norm(x, g[2]), ffn_w_gate[i], ffn_w_up[i], ffn_w_down[i])
        x = x + rms_norm(f, g[3])
    return x


def setup_inputs(seed: int = 0) -> dict:
    key = jax.random.key(seed)
    ks = jax.random.split(key, 16)

    def w(k, shape, fan_in):
        return jax.random.normal(k, shape, jnp.float32) * (fan_in ** -0.5)

    def gain(k, shape):
        return 1.0 + 0.02 * jax.random.normal(k, shape, jnp.float32)

    return {
        "x_prompt": jax.random.normal(ks[0], (BATCH, SEQ, D_MODEL), jnp.float32),
        "x_sample": jax.random.normal(ks[1], (DEC_BATCH, DEC_SEQ, D_MODEL), jnp.float32),
        "norm_g": gain(ks[2], (DEPTH, 4, D_MODEL)),
        "fnet_w_out": w(ks[3], (N_FOURIER_LAYERS, D_MODEL, D_MODEL), D_MODEL),
        "fnet_b_out": 0.02 * jax.random.normal(ks[4], (N_FOURIER_LAYERS, D_MODEL), jnp.float32),
        "mla_w_down": w(ks[5], (N_MLA_LAYERS, D_MODEL, Q_LORA_RANK + KV_LORA_RANK + QK_ROPE_DIM), D_MODEL),
        "mla_q_norm": gain(ks[6], (N_MLA_LAYERS, Q_LORA_RANK)),
        "mla_w_uq": w(ks[7], (N_MLA_LAYERS, Q_LORA_RANK, N_HEADS * (QK_NOPE_DIM + QK_ROPE_DIM)), Q_LORA_RANK),
        "mla_kv_norm": gain(ks[8], (N_MLA_LAYERS, KV_LORA_RANK)),
        "mla_w_ukv": w(ks[9], (N_MLA_LAYERS, KV_LORA_RANK, N_HEADS * (QK_NOPE_DIM + V_HEAD_DIM)), KV_LORA_RANK),
        "mla_w_o": w(ks[10], (N_MLA_LAYERS, N_HEADS * V_HEAD_DIM, D_MODEL), N_HEADS * V_HEAD_DIM),
        "ffn_w_gate": w(ks[11], (DEPTH, D_MODEL, D_FF), D_MODEL),
        "ffn_w_up": w(ks[12], (DEPTH, D_MODEL, D_FF), D_MODEL),
        "ffn_w_down": w(ks[13], (DEPTH, D_FF, D_MODEL), D_FF),
    }


def reference(x_prompt, x_sample, norm_g, fnet_w_out, fnet_b_out, mla_w_down, mla_q_norm,
              mla_w_uq, mla_kv_norm, mla_w_ukv, mla_w_o, ffn_w_gate, ffn_w_up, ffn_w_down):
    y_prompt = trunk(x_prompt, norm_g, fnet_w_out, fnet_b_out, mla_w_down, mla_q_norm, mla_w_uq,
                     mla_kv_norm, mla_w_ukv, mla_w_o, ffn_w_gate, ffn_w_up, ffn_w_down)
    y_sample = trunk(x_sample, norm_g, fnet_w_out, fnet_b_out, mla_w_down, mla_q_norm, mla_w_uq,
                     mla_kv_norm, mla_w_ukv, mla_w_o, ffn_w_gate, ffn_w_up, ffn_w_down)
    return (y_prompt, y_sample)
```

```python
import functools
import math

import numpy as np
import jax
import jax.numpy as jnp
from jax import lax
from jax.experimental import pallas as pl
from jax.experimental.pallas import tpu as pltpu

F32 = jnp.float32
BF16 = jnp.bfloat16

N_GROUPS = 4
N_HEADS = 8
NOPE = 128
ROPE = 64
VDIM = 128
Q_LORA = 256
KV_LORA = 256
QK_PAD = 256
ROPE_THETA = 10000.0
NORM_EPS = 1e-6
Q_SCALE = math.log2(math.e) / math.sqrt(NOPE + ROPE)

V7X_VMEM_LIMIT_BYTES = 56 * 1024 * 1024
ROW_TILE = 512
Q_TILE = 512
KV_TILE = 512
FF_CHUNK = 256
FOURIER_ROWS = 512

NT_DIMS = (((1,), (1,)), ((), ()))


def _rms(x, g):
    ms = jnp.mean(x * x, axis=-1, keepdims=True)
    return x * lax.rsqrt(ms + NORM_EPS) * g


def _const_spec(shape):
    return pl.BlockSpec(shape, lambda *_: (0,) * len(shape), pipeline_mode=pl.Buffered(1))


def _params(n_axes):
    return pltpu.CompilerParams(dimension_semantics=("parallel",) * n_axes,
                                vmem_limit_bytes=V7X_VMEM_LIMIT_BYTES)


def _split_seq(s):
    lg = int(round(math.log2(s)))
    assert 2 ** lg == s
    n1 = 2 ** ((lg + 1) // 2)
    return n1, s // n1


def _dft_tables(n1, n2):
    n = n1 * n2
    k2 = np.arange(n2)
    a2 = 2.0 * np.pi * np.outer(k2, k2) / n2
    f2 = np.concatenate([np.cos(a2), -np.sin(a2)], axis=0)
    k1 = np.arange(n1)
    a1 = 2.0 * np.pi * np.outer(k1, k1) / n1
    c1, s1 = np.cos(a1), np.sin(a1)
    g1 = np.block([[c1, s1], [-s1, c1]])
    at = 2.0 * np.pi * np.outer(k2, np.arange(n1)) / n
    return (f2.astype(np.float32), g1.astype(np.float32),
            np.cos(at).astype(np.float32), np.sin(at).astype(np.float32))


def _fourier_a_kernel(x_ref, g_ref, f2_ref, twc_ref, tws_ref, out_ref, *, n2, kc, c):
    f2 = f2_ref[...]
    g = g_ref[0:1, :]
    for jj in range(kc):
        xs = x_ref[:, jj * c:(jj + 1) * c]
        xn = _rms(xs, g).astype(BF16)
        t = jnp.dot(f2, xn, preferred_element_type=F32)
        tr, ti = t[:n2], t[n2:]
        cw = twc_ref[:, jj:jj + 1]
        sw = tws_ref[:, jj:jj + 1]
        out_ref[0, jj] = (tr * cw + ti * sw).astype(BF16)
        out_ref[1, jj] = (ti * cw - tr * sw).astype(BF16)


def _fourier_a(x, g, n1, n2, kc):
    b, s, c = x.shape
    f2, _, twc, tws = _dft_tables(n1, n2)
    f2 = jnp.asarray(f2).astype(BF16)
    twc = jnp.asarray(twc).reshape(n2, n1 // kc, kc).transpose(1, 0, 2)
    tws = jnp.asarray(tws).reshape(n2, n1 // kc, kc).transpose(1, 0, 2)
    xv = x.reshape(b, n2, n1 * c)
    kern = functools.partial(_fourier_a_kernel, n2=n2, kc=kc, c=c)
    return pl.pallas_call(
        kern,
        grid=(b, n1 // kc),
        in_specs=[
            pl.BlockSpec((None, n2, kc * c), lambda i, j: (i, 0, j)),
            _const_spec(g.shape),
            _const_spec(f2.shape),
            pl.BlockSpec((None, n2, kc), lambda i, j: (j, 0, 0)),
            pl.BlockSpec((None, n2, kc), lambda i, j: (j, 0, 0)),
        ],
        out_specs=pl.BlockSpec((None, 2, kc, n2, c), lambda i, j: (i, 0, j, 0, 0)),
        out_shape=jax.ShapeDtypeStruct((b, 2, n1, n2, c), BF16),
        compiler_params=_params(2),
        name="fourier_a",
    )(xv, g, f2, twc, tws)


def _fourier_b_kernel(t_ref, x_ref, g1_ref, wf_ref, b_ref, g_ref, out_ref, u_scr, *, n1, kc, c):
    g1 = g1_ref[...]
    for jj in range(kc):
        tr = t_ref[0, :, jj * c:(jj + 1) * c]
        ti = t_ref[1, :, jj * c:(jj + 1) * c]
        u = (jnp.dot(g1[:, :n1], tr, preferred_element_type=F32)
             + jnp.dot(g1[:, n1:], ti, preferred_element_type=F32))
        u_scr[jj * n1:(jj + 1) * n1, :c] = u[:n1].astype(BF16)
        u_scr[jj * n1:(jj + 1) * n1, c:] = u[n1:].astype(BF16)
    m = jnp.dot(u_scr[...], wf_ref[...], preferred_element_type=F32) + b_ref[...]
    y = _rms(m, g_ref[1:2, :])
    for jj in range(kc):
        out_ref[:, jj * c:(jj + 1) * c] = x_ref[:, jj * c:(jj + 1) * c] + y[jj * n1:(jj + 1) * n1]


def _fourier_b(tp, x, wf, bias, g, n1, n2, kc):
    b, s, c = x.shape
    _, g1, _, _ = _dft_tables(n1, n2)
    g1 = jnp.asarray(g1).astype(BF16)
    tv = tp.reshape(b, 2, n1, n2 * c)
    xv = x.reshape(b, n1, n2 * c)
    kern = functools.partial(_fourier_b_kernel, n1=n1, kc=kc, c=c)
    out = pl.pallas_call(
        kern,
        grid=(b, n2 // kc),
        in_specs=[
            pl.BlockSpec((None, 2, n1, kc * c), lambda i, j: (i, 0, 0, j)),
            pl.BlockSpec((None, n1, kc * c), lambda i, j: (i, 0, j)),
            _const_spec(g1.shape),
            _const_spec(wf.shape),
            _const_spec(bias.shape),
            _const_spec(g.shape),
        ],
        out_specs=pl.BlockSpec((None, n1, kc * c), lambda i, j: (i, 0, j)),
        out_shape=jax.ShapeDtypeStruct(xv.shape, F32),
        scratch_shapes=[pltpu.VMEM((kc * n1, 2 * c), BF16)],
        compiler_params=_params(2),
        name="fourier_b",
    )(tv, xv, g1, wf, bias, g)
    return out.reshape(b, s, c)


def _fold_kernel(cc_ref, sc_ref, w_ref, out_ref):
    w = w_ref[...]
    out_ref[0] = jnp.dot(cc_ref[...], w, preferred_element_type=F32,
                         precision=lax.Precision.HIGHEST).astype(BF16)
    out_ref[1] = jnp.dot(sc_ref[...], w, preferred_element_type=F32,
                         precision=lax.Precision.HIGHEST).astype(BF16)


def _fold_channel_dft(w_out):
    d = w_out.shape[0]
    gd = d // N_GROUPS
    idx = np.arange(gd)
    ang = 2.0 * np.pi * np.outer(idx, idx) / gd
    cc = jnp.asarray(np.cos(ang).astype(np.float32))
    sc = jnp.asarray(np.sin(ang).astype(np.float32))
    out = pl.pallas_call(
        _fold_kernel,
        grid=(N_GROUPS,),
        in_specs=[
            pl.BlockSpec((gd, gd), lambda i: (0, 0)),
            pl.BlockSpec((gd, gd), lambda i: (0, 0)),
            pl.BlockSpec((gd, d), lambda i: (i, 0)),
        ],
        out_specs=pl.BlockSpec((2, gd, d), lambda i: (0, i, 0)),
        out_shape=jax.ShapeDtypeStruct((2, d, d), BF16),
        compiler_params=_params(1),
        name="fold_channel_dft",
    )(cc, sc, w_out)
    return out.reshape(2 * d, d)


def _ffn_kernel(*refs, with_attn, d_ff):
    if with_attn:
        x_ref, o_ref, wo_ref, g_ref, wg_ref, wu_ref, wd_ref, out_ref, a_scr = refs
    else:
        x_ref, g_ref, wg_ref, wu_ref, wd_ref, out_ref, a_scr = refs
    x = x_ref[...]
    if with_attn:
        m = jnp.dot(o_ref[...], wo_ref[...], preferred_element_type=F32)
        x = x + _rms(m, g_ref[1:2, :])
    hn = _rms(x, g_ref[2:3, :]).astype(BF16)
    for f in range(0, d_ff, FF_CHUNK):
        gate = jnp.dot(hn, wg_ref[:, f:f + FF_CHUNK], preferred_element_type=F32)
        up = jnp.dot(hn, wu_ref[:, f:f + FF_CHUNK], preferred_element_type=F32)
        a_scr[:, f:f + FF_CHUNK] = (gate * jax.nn.sigmoid(gate) * up).astype(BF16)
    y = jnp.dot(a_scr[...], wd_ref[...], preferred_element_type=F32)
    out_ref[...] = x + _rms(y, g_ref[3:4, :])


def _ffn(x, g, wg, wu, wd, attn=None, wo=None):
    b, s, c = x.shape
    d_ff = wg.shape[1]
    assert d_ff % FF_CHUNK == 0 and s % ROW_TILE == 0
    tm = ROW_TILE
    row = lambda i, j: (i, j, 0)
    with_attn = attn is not None
    in_specs = [pl.BlockSpec((None, tm, c), row)]
    args = [x]
    if with_attn:
        in_specs += [pl.BlockSpec((None, tm, c), row), _const_spec(wo.shape)]
        args += [attn, wo]
    in_specs += [_const_spec(g.shape), _const_spec(wg.shape), _const_spec(wu.shape),
                 _const_spec(wd.shape)]
    args += [g, wg, wu, wd]
    kern = functools.partial(_ffn_kernel, with_attn=with_attn, d_ff=d_ff)
    return pl.pallas_call(
        kern,
        grid=(b, s // tm),
        in_specs=in_specs,
        out_specs=pl.BlockSpec((None, tm, c), row),
        out_shape=jax.ShapeDtypeStruct(x.shape, F32),
        scratch_shapes=[pltpu.VMEM((tm, d_ff), BF16)],
        compiler_params=_params(2),
        name="ffn_attn" if with_attn else "ffn",
    )(*args)


def _mla_proj_kernel(x_ref, g_ref, wdn_ref, qn_ref, kvn_ref, wuqt_ref, wuk_ref, wuvt_ref,
                     cosf_ref, sinf_ref, cost_ref, sint_ref, k_out, q_out, v_out):
    tm = x_ref.shape[0]
    xn = _rms(x_ref[...], g_ref[0:1, :]).astype(BF16)
    down = jnp.dot(xn, wdn_ref[...], preferred_element_type=F32)
    c_q = down[:, :Q_LORA]
    c_kv = down[:, Q_LORA:Q_LORA + KV_LORA]
    kr = down[:, Q_LORA + KV_LORA:Q_LORA + KV_LORA + ROPE]
    kr_swapped = down[:, Q_LORA + KV_LORA + ROPE:]
    k_rope = kr * cosf_ref[...] + kr_swapped * sinf_ref[...]
    k_tail = jnp.concatenate([k_rope, jnp.zeros_like(k_rope)], axis=1).astype(BF16)
    cqn = _rms(c_q, qn_ref[...]).astype(BF16)
    ckvn = _rms(c_kv, kvn_ref[...]).astype(BF16)

    k_nope = jnp.dot(ckvn, wuk_ref[...], preferred_element_type=F32)
    for h in range(N_HEADS):
        k_out[h, :, :NOPE] = k_nope[:, h * NOPE:(h + 1) * NOPE].astype(BF16)
        k_out[h, :, NOPE:] = k_tail

    qt = lax.dot_general(wuqt_ref[...], cqn, NT_DIMS, preferred_element_type=F32) * Q_SCALE
    cos_t = cost_ref[...]
    sin_t = sint_ref[...]
    half = ROPE // 2
    hd = NOPE + ROPE
    for h in range(N_HEADS):
        base = h * hd
        q_out[h, :NOPE, :] = qt[base:base + NOPE].astype(BF16)
        x1 = qt[base + NOPE:base + NOPE + half]
        x2 = qt[base + NOPE + half:base + hd]
        q_out[h, NOPE:NOPE + half, :] = (x1 * cos_t - x2 * sin_t).astype(BF16)
        q_out[h, NOPE + half:hd, :] = (x2 * cos_t + x1 * sin_t).astype(BF16)
        q_out[h, hd:, :] = jnp.zeros((QK_PAD - hd, tm), BF16)

    vt = lax.dot_general(wuvt_ref[...], ckvn, NT_DIMS, preferred_element_type=F32)
    for h in range(N_HEADS):
        v_out[h, 0] = vt[h * VDIM:(h + 1) * VDIM].astype(BF16)


def _mla_proj(x, g, wdn, qn, kvn, wuqt, wuk, wuvt, cosf, sinf, cos_t, sin_t):
    b, s, c = x.shape
    tm = KV_TILE
    assert s % tm == 0
    return pl.pallas_call(
        _mla_proj_kernel,
        grid=(b, s // tm),
        in_specs=[
            pl.BlockSpec((None, tm, c), lambda i, j: (i, j, 0)),
            _const_spec(g.shape), _const_spec(wdn.shape), _const_spec(qn.shape),
            _const_spec(kvn.shape), _const_spec(wuqt.shape), _const_spec(wuk.shape),
            _const_spec(wuvt.shape),
            pl.BlockSpec((tm, ROPE), lambda i, j: (j, 0)),
            pl.BlockSpec((tm, ROPE), lambda i, j: (j, 0)),
            pl.BlockSpec((ROPE // 2, tm), lambda i, j: (0, j)),
            pl.BlockSpec((ROPE // 2, tm), lambda i, j: (0, j)),
        ],
        out_specs=[
            pl.BlockSpec((None, N_HEADS, tm, QK_PAD), lambda i, j: (i, 0, j, 0)),
            pl.BlockSpec((None, N_HEADS, QK_PAD, tm), lambda i, j: (i, 0, 0, j)),
            pl.BlockSpec((None, N_HEADS, 1, VDIM, tm), lambda i, j: (i, 0, j, 0, 0)),
        ],
        out_shape=[
            jax.ShapeDtypeStruct((b, N_HEADS, s, QK_PAD), BF16),
            jax.ShapeDtypeStruct((b, N_HEADS, QK_PAD, s), BF16),
            jax.ShapeDtypeStruct((b, N_HEADS, s // tm, VDIM, tm), BF16),
        ],
        compiler_params=_params(2),
        name="mla_proj",
    )(x, g, wdn, qn, kvn, wuqt, wuk, wuvt, cosf, sinf, cos_t, sin_t)


def _attn_kernel(q_ref, k_ref, v_ref, o_ref, acc_ref, *, nk, tk):
    qt = q_ref[...]
    tq = qt.shape[1]
    acc_ref[...] = jnp.zeros_like(acc_ref)

    def body(i, carry):
        m, l = carry
        k = k_ref[pl.ds(pl.multiple_of(i * tk, tk), tk), :]
        s = jnp.dot(k, qt, preferred_element_type=F32)
        m_new = jnp.maximum(m, jnp.max(s, axis=0, keepdims=True))
        alpha = jnp.exp2(m - m_new)
        p = jnp.exp2(s - m_new)
        l = alpha * l + jnp.sum(p, axis=0, keepdims=True)
        pv = jnp.dot(v_ref[i], p.astype(BF16), preferred_element_type=F32)
        acc_ref[...] = alpha * acc_ref[...] + pv
        return m_new, l

    m0 = jnp.full((1, tq), -jnp.inf, F32)
    l0 = jnp.zeros((1, tq), F32)
    _, l = lax.fori_loop(0, nk, body, (m0, l0))
    out = acc_ref[...] / l
    o_ref[...] = out.T.astype(BF16)


def _attention(qt, kc, vt):
    b, h, s, _ = kc.shape
    tq, tk = Q_TILE, KV_TILE
    nk = s // tk
    kern = functools.partial(_attn_kernel, nk=nk, tk=tk)
    return pl.pallas_call(
        kern,
        grid=(b, h, s // tq),
        in_specs=[
            pl.BlockSpec((None, None, QK_PAD, tq), lambda i, j, q: (i, j, 0, q)),
            pl.BlockSpec((None, None, s, QK_PAD), lambda i, j, q: (i, j, 0, 0)),
            pl.BlockSpec((None, None, nk, VDIM, tk), lambda i, j, q: (i, j, 0, 0, 0)),
        ],
        out_specs=pl.BlockSpec((None, tq, VDIM), lambda i, j, q: (i, q, j)),
        out_shape=jax.ShapeDtypeStruct((b, s, h * VDIM), BF16),
        scratch_shapes=[pltpu.VMEM((VDIM, tq), F32)],
        compiler_params=_params(3),
        name="attention",
    )(qt, kc, vt)


def _rope_tables(s):
    inv_freq = 1.0 / (ROPE_THETA ** (jnp.arange(0, ROPE, 2, dtype=F32) / ROPE))
    ang = jnp.arange(s, dtype=F32)[:, None] * inv_freq[None, :]
    cos, sin = jnp.cos(ang), jnp.sin(ang)
    cosf = jnp.concatenate([cos, cos], axis=1)
    sinf = jnp.concatenate([-sin, sin], axis=1)
    return cosf, sinf, cos.T, sin.T


def _prep_weights(fnet_w_out, mla_w_down, mla_w_uq, mla_w_ukv, mla_w_o,
                  ffn_w_gate, ffn_w_up, ffn_w_down):
    w = {}
    w["wf"] = _fold_channel_dft(fnet_w_out[0])
    wd = mla_w_down[0]
    r0 = Q_LORA + KV_LORA
    half = ROPE // 2
    w["wdn"] = jnp.concatenate([wd, wd[:, r0 + half:r0 + ROPE], wd[:, r0:r0 + half]],
                               axis=1).astype(BF16)
    w["wuqt"] = mla_w_uq[0].T.astype(BF16)
    wukv = mla_w_ukv[0].reshape(KV_LORA, N_HEADS, NOPE + VDIM)
    w["wuk"] = wukv[:, :, :NOPE].reshape(KV_LORA, N_HEADS * NOPE).astype(BF16)
    w["wuvt"] = wukv[:, :, NOPE:].reshape(KV_LORA, N_HEADS * VDIM).T.astype(BF16)
    w["wo"] = mla_w_o[0].astype(BF16)
    w["wg"] = ffn_w_gate.astype(BF16)
    w["wu"] = ffn_w_up.astype(BF16)
    w["wd"] = ffn_w_down.astype(BF16)
    return w


def _trunk(x, norm_g, fnet_b_out, mla_q_norm, mla_kv_norm, w):
    b, s, c = x.shape
    n1, n2 = _split_seq(s)
    kc = FOURIER_ROWS // n1
    g0, g1 = norm_g[0], norm_g[1]
    tp = _fourier_a(x, g0, n1, n2, kc)
    x = _fourier_b(tp, x, w["wf"], fnet_b_out[0][None, :], g0, n1, n2, kc)
    x = _ffn(x, g0, w["wg"][0], w["wu"][0], w["wd"][0])
    cosf, sinf, cos_t, sin_t = _rope_tables(s)
    kc_, qt, vt = _mla_proj(x, g1, w["wdn"], mla_q_norm[0][None, :], mla_kv_norm[0][None, :],
                            w["wuqt"], w["wuk"], w["wuvt"], cosf, sinf, cos_t, sin_t)
    o = _attention(qt, kc_, vt)
    return _ffn(x, g1, w["wg"][1], w["wu"][1], w["wd"][1], attn=o, wo=w["wo"])


def kernel(x_prompt, x_sample, norm_g, fnet_w_out, fnet_b_out, mla_w_down, mla_q_norm, mla_w_uq,
           mla_kv_norm, mla_w_ukv, mla_w_o, ffn_w_gate, ffn_w_up, ffn_w_down):
    w = _prep_weights(fnet_w_out, mla_w_down, mla_w_uq, mla_w_ukv, mla_w_o,
                      ffn_w_gate, ffn_w_up, ffn_w_down)
    y_prompt = _trunk(x_prompt, norm_g, fnet_b_out, mla_q_norm, mla_kv_norm, w)
    y_sample = _trunk(x_sample, norm_g, fnet_b_out, mla_q_norm, mla_kv_norm, w)
    return (y_prompt, y_sample)
```

```python
import functools
import math

import numpy as np
import jax
import jax.numpy as jnp
from jax import lax
from jax.experimental import pallas as pl
from jax.experimental.pallas import tpu as pltpu

F32 = jnp.float32
BF16 = jnp.bfloat16

N_GROUPS = 4
N_HEADS = 8
NOPE = 128
ROPE = 64
VDIM = 128
Q_LORA = 256
KV_LORA = 256
QK_PAD = 256
ROPE_THETA = 10000.0
NORM_EPS = 1e-6
Q_SCALE = math.log2(math.e) / math.sqrt(NOPE + ROPE)

V7X_VMEM_LIMIT_BYTES = 56 * 1024 * 1024
ROW_TILE = 512
Q_TILE = 512
KV_TILE = 512
SCORE_LOOKAHEAD = 4
FF_CHUNK = 256
FOURIER_ROWS = 512

NT_DIMS = (((1,), (1,)), ((), ()))


def _rms(x, g):
    ms = jnp.mean(x * x, axis=-1, keepdims=True)
    return x * lax.rsqrt(ms + NORM_EPS) * g


def _const_spec(shape):
    return pl.BlockSpec(shape, lambda *_: (0,) * len(shape), pipeline_mode=pl.Buffered(1))


def _params(n_axes):
    return pltpu.CompilerParams(dimension_semantics=("parallel",) * n_axes,
                                vmem_limit_bytes=V7X_VMEM_LIMIT_BYTES)


def _split_seq(s):
    lg = int(round(math.log2(s)))
    assert 2 ** lg == s
    n1 = 2 ** ((lg + 1) // 2)
    return n1, s // n1


def _dft_tables(n1, n2):
    n = n1 * n2
    k2 = np.arange(n2)
    a2 = 2.0 * np.pi * np.outer(k2, k2) / n2
    f2 = np.concatenate([np.cos(a2), -np.sin(a2)], axis=0)
    k1 = np.arange(n1)
    a1 = 2.0 * np.pi * np.outer(k1, k1) / n1
    c1, s1 = np.cos(a1), np.sin(a1)
    g1 = np.block([[c1, s1], [-s1, c1]])
    at = 2.0 * np.pi * np.outer(k2, np.arange(n1)) / n
    return (f2.astype(np.float32), g1.astype(np.float32),
            np.cos(at).astype(np.float32), np.sin(at).astype(np.float32))


def _fourier_a_kernel(x_ref, g_ref, f2_ref, twc_ref, tws_ref, out_ref, *, n2, kc, c):
    f2 = f2_ref[...]
    g = g_ref[0:1, :]
    for jj in range(kc):
        xs = x_ref[:, jj * c:(jj + 1) * c]
        xn = _rms(xs, g).astype(BF16)
        t = jnp.dot(f2, xn, preferred_element_type=F32)
        tr, ti = t[:n2], t[n2:]
        cw = twc_ref[:, jj:jj + 1]
        sw = tws_ref[:, jj:jj + 1]
        out_ref[0, jj] = (tr * cw + ti * sw).astype(BF16)
        out_ref[1, jj] = (ti * cw - tr * sw).astype(BF16)


def _fourier_a(x, g, n1, n2, kc):
    b, s, c = x.shape
    f2, _, twc, tws = _dft_tables(n1, n2)
    f2 = jnp.asarray(f2).astype(BF16)
    twc = jnp.asarray(twc).reshape(n2, n1 // kc, kc).transpose(1, 0, 2)
    tws = jnp.asarray(tws).reshape(n2, n1 // kc, kc).transpose(1, 0, 2)
    xv = x.reshape(b, n2, n1 * c)
    kern = functools.partial(_fourier_a_kernel, n2=n2, kc=kc, c=c)
    return pl.pallas_call(
        kern,
        grid=(b, n1 // kc),
        in_specs=[
            pl.BlockSpec((None, n2, kc * c), lambda i, j: (i, 0, j)),
            _const_spec(g.shape),
            _const_spec(f2.shape),
            pl.BlockSpec((None, n2, kc), lambda i, j: (j, 0, 0)),
            pl.BlockSpec((None, n2, kc), lambda i, j: (j, 0, 0)),
        ],
        out_specs=pl.BlockSpec((None, 2, kc, n2, c), lambda i, j: (i, 0, j, 0, 0)),
        out_shape=jax.ShapeDtypeStruct((b, 2, n1, n2, c), BF16),
        compiler_params=_params(2),
        name="fourier_a",
    )(xv, g, f2, twc, tws)


def _fourier_b_kernel(t_ref, x_ref, g1_ref, wf_ref, b_ref, g_ref, out_ref, u_scr, *, n1, kc, c):
    g1 = g1_ref[...]
    for jj in range(kc):
        tr = t_ref[0, :, jj * c:(jj + 1) * c]
        ti = t_ref[1, :, jj * c:(jj + 1) * c]
        u = (jnp.dot(g1[:, :n1], tr, preferred_element_type=F32)
             + jnp.dot(g1[:, n1:], ti, preferred_element_type=F32))
        u_scr[jj * n1:(jj + 1) * n1, :c] = u[:n1].astype(BF16)
        u_scr[jj * n1:(jj + 1) * n1, c:] = u[n1:].astype(BF16)
    m = jnp.dot(u_scr[...], wf_ref[...], preferred_element_type=F32) + b_ref[...]
    y = _rms(m, g_ref[1:2, :])
    for jj in range(kc):
        out_ref[:, jj * c:(jj + 1) * c] = x_ref[:, jj * c:(jj + 1) * c] + y[jj * n1:(jj + 1) * n1]


def _fourier_b(tp, x, wf, bias, g, n1, n2, kc):
    b, s, c = x.shape
    _, g1, _, _ = _dft_tables(n1, n2)
    g1 = jnp.asarray(g1).astype(BF16)
    tv = tp.reshape(b, 2, n1, n2 * c)
    xv = x.reshape(b, n1, n2 * c)
    kern = functools.partial(_fourier_b_kernel, n1=n1, kc=kc, c=c)
    out = pl.pallas_call(
        kern,
        grid=(b, n2 // kc),
        in_specs=[
            pl.BlockSpec((None, 2, n1, kc * c), lambda i, j: (i, 0, 0, j)),
            pl.BlockSpec((None, n1, kc * c), lambda i, j: (i, 0, j)),
            _const_spec(g1.shape),
            _const_spec(wf.shape),
            _const_spec(bias.shape),
            _const_spec(g.shape),
        ],
        out_specs=pl.BlockSpec((None, n1, kc * c), lambda i, j: (i, 0, j)),
        out_shape=jax.ShapeDtypeStruct(xv.shape, F32),
        scratch_shapes=[pltpu.VMEM((kc * n1, 2 * c), BF16)],
        compiler_params=_params(2),
        name="fourier_b",
    )(tv, xv, g1, wf, bias, g)
    return out.reshape(b, s, c)


def _fold_kernel(cc_ref, sc_ref, w_ref, out_ref):
    w = w_ref[...]
    out_ref[0] = jnp.dot(cc_ref[...], w, preferred_element_type=F32,
                         precision=lax.Precision.HIGHEST).astype(BF16)
    out_ref[1] = jnp.dot(sc_ref[...], w, preferred_element_type=F32,
                         precision=lax.Precision.HIGHEST).astype(BF16)


def _fold_channel_dft(w_out):
    d = w_out.shape[0]
    gd = d // N_GROUPS
    idx = np.arange(gd)
    ang = 2.0 * np.pi * np.outer(idx, idx) / gd
    cc = jnp.asarray(np.cos(ang).astype(np.float32))
    sc = jnp.asarray(np.sin(ang).astype(np.float32))
    out = pl.pallas_call(
        _fold_kernel,
        grid=(N_GROUPS,),
        in_specs=[
            pl.BlockSpec((gd, gd), lambda i: (0, 0)),
            pl.BlockSpec((gd, gd), lambda i: (0, 0)),
            pl.BlockSpec((gd, d), lambda i: (i, 0)),
        ],
        out_specs=pl.BlockSpec((2, gd, d), lambda i: (0, i, 0)),
        out_shape=jax.ShapeDtypeStruct((2, d, d), BF16),
        compiler_params=_params(1),
        name="fold_channel_dft",
    )(cc, sc, w_out)
    return out.reshape(2 * d, d)


def _ffn_kernel(*refs, with_attn, d_ff):
    if with_attn:
        x_ref, o_ref, wo_ref, g_ref, wg_ref, wu_ref, wd_ref, out_ref, a_scr = refs
    else:
        x_ref, g_ref, wg_ref, wu_ref, wd_ref, out_ref, a_scr = refs
    x = x_ref[...]
    if with_attn:
        m = jnp.dot(o_ref[...], wo_ref[...], preferred_element_type=F32)
        x = x + _rms(m, g_ref[1:2, :])
    hn = _rms(x, g_ref[2:3, :]).astype(BF16)
    for f in range(0, d_ff, FF_CHUNK):
        gate = jnp.dot(hn, wg_ref[:, f:f + FF_CHUNK], preferred_element_type=F32)
        up = jnp.dot(hn, wu_ref[:, f:f + FF_CHUNK], preferred_element_type=F32)
        a_scr[:, f:f + FF_CHUNK] = (gate * jax.nn.sigmoid(gate) * up).astype(BF16)
    y = jnp.dot(a_scr[...], wd_ref[...], preferred_element_type=F32)
    out_ref[...] = x + _rms(y, g_ref[3:4, :])


def _ffn(x, g, wg, wu, wd, attn=None, wo=None):
    b, s, c = x.shape
    d_ff = wg.shape[1]
    assert d_ff % FF_CHUNK == 0 and s % ROW_TILE == 0
    tm = ROW_TILE
    row = lambda i, j: (i, j, 0)
    with_attn = attn is not None
    in_specs = [pl.BlockSpec((None, tm, c), row)]
    args = [x]
    if with_attn:
        in_specs += [pl.BlockSpec((None, tm, c), row), _const_spec(wo.shape)]
        args += [attn, wo]
    in_specs += [_const_spec(g.shape), _const_spec(wg.shape), _const_spec(wu.shape),
                 _const_spec(wd.shape)]
    args += [g, wg, wu, wd]
    kern = functools.partial(_ffn_kernel, with_attn=with_attn, d_ff=d_ff)
    return pl.pallas_call(
        kern,
        grid=(b, s // tm),
        in_specs=in_specs,
        out_specs=pl.BlockSpec((None, tm, c), row),
        out_shape=jax.ShapeDtypeStruct(x.shape, F32),
        scratch_shapes=[pltpu.VMEM((tm, d_ff), BF16)],
        compiler_params=_params(2),
        name="ffn_attn" if with_attn else "ffn",
    )(*args)


def _mla_proj_kernel(x_ref, g_ref, wdn_ref, qn_ref, kvn_ref, wuqt_ref, wuk_ref, wuvt_ref,
                     cosf_ref, sinf_ref, cost_ref, sint_ref, k_out, q_out, v_out):
    tm = x_ref.shape[0]
    xn = _rms(x_ref[...], g_ref[0:1, :]).astype(BF16)
    down = jnp.dot(xn, wdn_ref[...], preferred_element_type=F32)
    c_q = down[:, :Q_LORA]
    c_kv = down[:, Q_LORA:Q_LORA + KV_LORA]
    kr = down[:, Q_LORA + KV_LORA:Q_LORA + KV_LORA + ROPE]
    kr_swapped = down[:, Q_LORA + KV_LORA + ROPE:]
    k_rope = kr * cosf_ref[...] + kr_swapped * sinf_ref[...]
    k_tail = jnp.concatenate([k_rope, jnp.zeros_like(k_rope)], axis=1).astype(BF16)
    cqn = _rms(c_q, qn_ref[...]).astype(BF16)
    ckvn = _rms(c_kv, kvn_ref[...]).astype(BF16)

    k_nope = jnp.dot(ckvn, wuk_ref[...], preferred_element_type=F32)
    for h in range(N_HEADS):
        k_out[h, :, :NOPE] = k_nope[:, h * NOPE:(h + 1) * NOPE].astype(BF16)
        k_out[h, :, NOPE:] = k_tail

    qt = lax.dot_general(wuqt_ref[...], cqn, NT_DIMS, preferred_element_type=F32) * Q_SCALE
    cos_t = cost_ref[...]
    sin_t = sint_ref[...]
    half = ROPE // 2
    hd = NOPE + ROPE
    for h in range(N_HEADS):
        base = h * hd
        q_out[h, :NOPE, :] = qt[base:base + NOPE].astype(BF16)
        x1 = qt[base + NOPE:base + NOPE + half]
        x2 = qt[base + NOPE + half:base + hd]
        q_out[h, NOPE:NOPE + half, :] = (x1 * cos_t - x2 * sin_t).astype(BF16)
        q_out[h, NOPE + half:hd, :] = (x2 * cos_t + x1 * sin_t).astype(BF16)
        q_out[h, hd:, :] = jnp.zeros((QK_PAD - hd, tm), BF16)

    vt = lax.dot_general(wuvt_ref[...], ckvn, NT_DIMS, preferred_element_type=F32)
    for h in range(N_HEADS):
        v_out[h, 0] = vt[h * VDIM:(h + 1) * VDIM].astype(BF16)


def _mla_proj(x, g, wdn, qn, kvn, wuqt, wuk, wuvt, cosf, sinf, cos_t, sin_t):
    b, s, c = x.shape
    tm = KV_TILE
    assert s % tm == 0
    return pl.pallas_call(
        _mla_proj_kernel,
        grid=(b, s // tm),
        in_specs=[
            pl.BlockSpec((None, tm, c), lambda i, j: (i, j, 0)),
            _const_spec(g.shape), _const_spec(wdn.shape), _const_spec(qn.shape),
            _const_spec(kvn.shape), _const_spec(wuqt.shape), _const_spec(wuk.shape),
            _const_spec(wuvt.shape),
            pl.BlockSpec((tm, ROPE), lambda i, j: (j, 0)),
            pl.BlockSpec((tm, ROPE), lambda i, j: (j, 0)),
            pl.BlockSpec((ROPE // 2, tm), lambda i, j: (0, j)),
            pl.BlockSpec((ROPE // 2, tm), lambda i, j: (0, j)),
        ],
        out_specs=[
            pl.BlockSpec((None, N_HEADS, tm, QK_PAD), lambda i, j: (i, 0, j, 0)),
            pl.BlockSpec((None, N_HEADS, QK_PAD, tm), lambda i, j: (i, 0, 0, j)),
            pl.BlockSpec((None, N_HEADS, 1, VDIM, tm), lambda i, j: (i, 0, j, 0, 0)),
        ],
        out_shape=[
            jax.ShapeDtypeStruct((b, N_HEADS, s, QK_PAD), BF16),
            jax.ShapeDtypeStruct((b, N_HEADS, QK_PAD, s), BF16),
            jax.ShapeDtypeStruct((b, N_HEADS, s // tm, VDIM, tm), BF16),
        ],
        compiler_params=_params(2),
        name="mla_proj",
    )(x, g, wdn, qn, kvn, wuqt, wuk, wuvt, cosf, sinf, cos_t, sin_t)


def _attn_kernel(q_ref, k_ref, v_ref, o_ref, acc_ref, s_ref, *, nk, tk, nb):
    qt = q_ref[...]
    tq = qt.shape[1]
    acc_ref[...] = jnp.zeros_like(acc_ref)

    def scores(i, slot):
        k = k_ref[pl.ds(pl.multiple_of(i * tk, tk), tk), :]
        s_ref[slot] = jnp.dot(k, qt, preferred_element_type=F32)

    def consume(i, slot, m, l):
        s = s_ref[slot]
        m_new = jnp.maximum(m, jnp.max(s, axis=0, keepdims=True))
        alpha = jnp.exp2(m - m_new)
        p = jnp.exp2(s - m_new)
        l = alpha * l + jnp.sum(p, axis=0, keepdims=True)
        pv = jnp.dot(v_ref[i], p.astype(BF16), preferred_element_type=F32)
        acc_ref[...] = alpha * acc_ref[...] + pv
        return m_new, l

    def half(base, read0, write0, m, l, prefetch):
        for u in range(nb):
            if prefetch:
                scores(base + nb + u, write0 + u)
            m, l = consume(base + u, read0 + u, m, l)
        return m, l

    def super_block(base, m, l, last):
        m, l = half(base, 0, nb, m, l, True)
        return half(base + nb, nb, 0, m, l, not last)

    n_super = nk // (2 * nb)
    for u in range(nb):
        scores(u, u)
    m = jnp.full((1, tq), -jnp.inf, F32)
    l = jnp.zeros((1, tq), F32)
    m, l = lax.fori_loop(
        0, n_super - 1,
        lambda j, c: super_block(j * (2 * nb), c[0], c[1], False), (m, l))
    m, l = super_block((n_super - 1) * (2 * nb), m, l, True)
    out = acc_ref[...] / l
    o_ref[...] = out.T.astype(BF16)


def _attention(qt, kc, vt):
    b, h, s, _ = kc.shape
    tq, tk = Q_TILE, KV_TILE
    nk = s // tk
    nb = SCORE_LOOKAHEAD
    assert nk % (2 * nb) == 0
    kern = functools.partial(_attn_kernel, nk=nk, tk=tk, nb=nb)
    return pl.pallas_call(
        kern,
        grid=(b, h, s // tq),
        in_specs=[
            pl.BlockSpec((None, None, QK_PAD, tq), lambda i, j, q: (i, j, 0, q)),
            pl.BlockSpec((None, None, s, QK_PAD), lambda i, j, q: (i, j, 0, 0)),
            pl.BlockSpec((None, None, nk, VDIM, tk), lambda i, j, q: (i, j, 0, 0, 0)),
        ],
        out_specs=pl.BlockSpec((None, tq, VDIM), lambda i, j, q: (i, q, j)),
        out_shape=jax.ShapeDtypeStruct((b, s, h * VDIM), BF16),
        scratch_shapes=[pltpu.VMEM((VDIM, tq), F32), pltpu.VMEM((2 * nb, tk, tq), F32)],
        compiler_params=_params(3),
        name="attention",
    )(qt, kc, vt)


def _rope_tables(s):
    inv_freq = 1.0 / (ROPE_THETA ** (jnp.arange(0, ROPE, 2, dtype=F32) / ROPE))
    ang = jnp.arange(s, dtype=F32)[:, None] * inv_freq[None, :]
    cos, sin = jnp.cos(ang), jnp.sin(ang)
    cosf = jnp.concatenate([cos, cos], axis=1)
    sinf = jnp.concatenate([-sin, sin], axis=1)
    return cosf, sinf, cos.T, sin.T


def _prep_weights(fnet_w_out, mla_w_down, mla_w_uq, mla_w_ukv, mla_w_o,
                  ffn_w_gate, ffn_w_up, ffn_w_down):
    w = {}
    w["wf"] = _fold_channel_dft(fnet_w_out[0])
    wd = mla_w_down[0]
    r0 = Q_LORA + KV_LORA
    half = ROPE // 2
    w["wdn"] = jnp.concatenate([wd, wd[:, r0 + half:r0 + ROPE], wd[:, r0:r0 + half]],
                               axis=1).astype(BF16)
    w["wuqt"] = mla_w_uq[0].T.astype(BF16)
    wukv = mla_w_ukv[0].reshape(KV_LORA, N_HEADS, NOPE + VDIM)
    w["wuk"] = wukv[:, :, :NOPE].reshape(KV_LORA, N_HEADS * NOPE).astype(BF16)
    w["wuvt"] = wukv[:, :, NOPE:].reshape(KV_LORA, N_HEADS * VDIM).T.astype(BF16)
    w["wo"] = mla_w_o[0].astype(BF16)
    w["wg"] = ffn_w_gate.astype(BF16)
    w["wu"] = ffn_w_up.astype(BF16)
    w["wd"] = ffn_w_down.astype(BF16)
    return w


def _trunk(x, norm_g, fnet_b_out, mla_q_norm, mla_kv_norm, w):
    b, s, c = x.shape
    n1, n2 = _split_seq(s)
    kc = FOURIER_ROWS // n1
    g0, g1 = norm_g[0], norm_g[1]
    tp = _fourier_a(x, g0, n1, n2, kc)
    x = _fourier_b(tp, x, w["wf"], fnet_b_out[0][None, :], g0, n1, n2, kc)
    x = _ffn(x, g0, w["wg"][0], w["wu"][0], w["wd"][0])
    cosf, sinf, cos_t, sin_t = _rope_tables(s)
    kc_, qt, vt = _mla_proj(x, g1, w["wdn"], mla_q_norm[0][None, :], mla_kv_norm[0][None, :],
                            w["wuqt"], w["wuk"], w["wuvt"], cosf, sinf, cos_t, sin_t)
    o = _attention(qt, kc_, vt)
    return _ffn(x, g1, w["wg"][1], w["wu"][1], w["wd"][1], attn=o, wo=w["wo"])


def kernel(x_prompt, x_sample, norm_g, fnet_w_out, fnet_b_out, mla_w_down, mla_q_norm, mla_w_uq,
           mla_kv_norm, mla_w_ukv, mla_w_o, ffn_w_gate, ffn_w_up, ffn_w_down):
    w = _prep_weights(fnet_w_out, mla_w_down, mla_w_uq, mla_w_ukv, mla_w_o,
                      ffn_w_gate, ffn_w_up, ffn_w_down)
    y_prompt = _trunk(x_prompt, norm_g, fnet_b_out, mla_q_norm, mla_kv_norm, w)
    y_sample = _trunk(x_sample, norm_g, fnet_b_out, mla_q_norm, mla_kv_norm, w)
    return (y_prompt, y_sample)
```

```python
import functools
import math

import numpy as np
import jax
import jax.numpy as jnp
from jax import lax
from jax.experimental import pallas as pl
from jax.experimental.pallas import tpu as pltpu

F32 = jnp.float32
BF16 = jnp.bfloat16

N_GROUPS = 4
N_HEADS = 8
NOPE = 128
ROPE = 64
VDIM = 128
V_ROWS = VDIM + 16
Q_LORA = 256
KV_LORA = 256
QK_PAD = 256
ROPE_THETA = 10000.0
NORM_EPS = 1e-6
Q_SCALE = math.log2(math.e) / math.sqrt(NOPE + ROPE)

V7X_VMEM_LIMIT_BYTES = 56 * 1024 * 1024
ROW_TILE = 512
Q_TILE = 512
KV_TILE = 512
SCORE_LOOKAHEAD = 4
FF_CHUNK = 256
FOURIER_ROWS = 512

NT_DIMS = (((1,), (1,)), ((), ()))


def _rms(x, g):
    ms = jnp.mean(x * x, axis=-1, keepdims=True)
    return x * lax.rsqrt(ms + NORM_EPS) * g


def _const_spec(shape):
    return pl.BlockSpec(shape, lambda *_: (0,) * len(shape), pipeline_mode=pl.Buffered(1))


def _params(n_axes):
    return pltpu.CompilerParams(dimension_semantics=("parallel",) * n_axes,
                                vmem_limit_bytes=V7X_VMEM_LIMIT_BYTES)


def _split_seq(s):
    lg = int(round(math.log2(s)))
    assert 2 ** lg == s
    n1 = 2 ** ((lg + 1) // 2)
    return n1, s // n1


def _dft_tables(n1, n2):
    n = n1 * n2
    k2 = np.arange(n2)
    a2 = 2.0 * np.pi * np.outer(k2, k2) / n2
    f2 = np.concatenate([np.cos(a2), -np.sin(a2)], axis=0)
    k1 = np.arange(n1)
    a1 = 2.0 * np.pi * np.outer(k1, k1) / n1
    c1, s1 = np.cos(a1), np.sin(a1)
    g1 = np.block([[c1, s1], [-s1, c1]])
    at = 2.0 * np.pi * np.outer(k2, np.arange(n1)) / n
    return (f2.astype(np.float32), g1.astype(np.float32),
            np.cos(at).astype(np.float32), np.sin(at).astype(np.float32))


def _fourier_a_kernel(x_ref, g_ref, f2_ref, twc_ref, tws_ref, out_ref, *, n2, kc, c):
    f2 = f2_ref[...]
    g = g_ref[0:1, :]
    for jj in range(kc):
        xs = x_ref[:, jj * c:(jj + 1) * c]
        xn = _rms(xs, g).astype(BF16)
        t = jnp.dot(f2, xn, preferred_element_type=F32)
        tr, ti = t[:n2], t[n2:]
        cw = twc_ref[:, jj:jj + 1]
        sw = tws_ref[:, jj:jj + 1]
        out_ref[0, jj] = (tr * cw + ti * sw).astype(BF16)
        out_ref[1, jj] = (ti * cw - tr * sw).astype(BF16)


def _fourier_a(x, g, n1, n2, kc):
    b, s, c = x.shape
    f2, _, twc, tws = _dft_tables(n1, n2)
    f2 = jnp.asarray(f2).astype(BF16)
    twc = jnp.asarray(twc).reshape(n2, n1 // kc, kc).transpose(1, 0, 2)
    tws = jnp.asarray(tws).reshape(n2, n1 // kc, kc).transpose(1, 0, 2)
    xv = x.reshape(b, n2, n1 * c)
    kern = functools.partial(_fourier_a_kernel, n2=n2, kc=kc, c=c)
    return pl.pallas_call(
        kern,
        grid=(b, n1 // kc),
        in_specs=[
            pl.BlockSpec((None, n2, kc * c), lambda i, j: (i, 0, j)),
            _const_spec(g.shape),
            _const_spec(f2.shape),
            pl.BlockSpec((None, n2, kc), lambda i, j: (j, 0, 0)),
            pl.BlockSpec((None, n2, kc), lambda i, j: (j, 0, 0)),
        ],
        out_specs=pl.BlockSpec((None, 2, kc, n2, c), lambda i, j: (i, 0, j, 0, 0)),
        out_shape=jax.ShapeDtypeStruct((b, 2, n1, n2, c), BF16),
        compiler_params=_params(2),
        name="fourier_a",
    )(xv, g, f2, twc, tws)


def _fourier_b_kernel(t_ref, x_ref, g1_ref, wf_ref, b_ref, g_ref, out_ref, u_scr, *, n1, kc, c):
    g1 = g1_ref[...]
    for jj in range(kc):
        tr = t_ref[0, :, jj * c:(jj + 1) * c]
        ti = t_ref[1, :, jj * c:(jj + 1) * c]
        u = (jnp.dot(g1[:, :n1], tr, preferred_element_type=F32)
             + jnp.dot(g1[:, n1:], ti, preferred_element_type=F32))
        u_scr[jj * n1:(jj + 1) * n1, :c] = u[:n1].astype(BF16)
        u_scr[jj * n1:(jj + 1) * n1, c:] = u[n1:].astype(BF16)
    m = jnp.dot(u_scr[...], wf_ref[...], preferred_element_type=F32) + b_ref[...]
    y = _rms(m, g_ref[1:2, :])
    for jj in range(kc):
        out_ref[:, jj * c:(jj + 1) * c] = x_ref[:, jj * c:(jj + 1) * c] + y[jj * n1:(jj + 1) * n1]


def _fourier_b(tp, x, wf, bias, g, n1, n2, kc):
    b, s, c = x.shape
    _, g1, _, _ = _dft_tables(n1, n2)
    g1 = jnp.asarray(g1).astype(BF16)
    tv = tp.reshape(b, 2, n1, n2 * c)
    xv = x.reshape(b, n1, n2 * c)
    kern = functools.partial(_fourier_b_kernel, n1=n1, kc=kc, c=c)
    out = pl.pallas_call(
        kern,
        grid=(b, n2 // kc),
        in_specs=[
            pl.BlockSpec((None, 2, n1, kc * c), lambda i, j: (i, 0, 0, j)),
            pl.BlockSpec((None, n1, kc * c), lambda i, j: (i, 0, j)),
            _const_spec(g1.shape),
            _const_spec(wf.shape),
            _const_spec(bias.shape),
            _const_spec(g.shape),
        ],
        out_specs=pl.BlockSpec((None, n1, kc * c), lambda i, j: (i, 0, j)),
        out_shape=jax.ShapeDtypeStruct(xv.shape, F32),
        scratch_shapes=[pltpu.VMEM((kc * n1, 2 * c), BF16)],
        compiler_params=_params(2),
        name="fourier_b",
    )(tv, xv, g1, wf, bias, g)
    return out.reshape(b, s, c)


def _fold_kernel(cc_ref, sc_ref, w_ref, out_ref):
    w = w_ref[...]
    out_ref[0] = jnp.dot(cc_ref[...], w, preferred_element_type=F32,
                         precision=lax.Precision.HIGHEST).astype(BF16)
    out_ref[1] = jnp.dot(sc_ref[...], w, preferred_element_type=F32,
                         precision=lax.Precision.HIGHEST).astype(BF16)


def _fold_channel_dft(w_out):
    d = w_out.shape[0]
    gd = d // N_GROUPS
    idx = np.arange(gd)
    ang = 2.0 * np.pi * np.outer(idx, idx) / gd
    cc = jnp.asarray(np.cos(ang).astype(np.float32))
    sc = jnp.asarray(np.sin(ang).astype(np.float32))
    out = pl.pallas_call(
        _fold_kernel,
        grid=(N_GROUPS,),
        in_specs=[
            pl.BlockSpec((gd, gd), lambda i: (0, 0)),
            pl.BlockSpec((gd, gd), lambda i: (0, 0)),
            pl.BlockSpec((gd, d), lambda i: (i, 0)),
        ],
        out_specs=pl.BlockSpec((2, gd, d), lambda i: (0, i, 0)),
        out_shape=jax.ShapeDtypeStruct((2, d, d), BF16),
        compiler_params=_params(1),
        name="fold_channel_dft",
    )(cc, sc, w_out)
    return out.reshape(2 * d, d)


def _ffn_kernel(*refs, with_attn, d_ff):
    if with_attn:
        x_ref, o_ref, wo_ref, g_ref, wg_ref, wu_ref, wd_ref, out_ref, a_scr = refs
    else:
        x_ref, g_ref, wg_ref, wu_ref, wd_ref, out_ref, a_scr = refs
    x = x_ref[...]
    if with_attn:
        m = jnp.dot(o_ref[...], wo_ref[...], preferred_element_type=F32)
        x = x + _rms(m, g_ref[1:2, :])
    hn = _rms(x, g_ref[2:3, :]).astype(BF16)
    for f in range(0, d_ff, FF_CHUNK):
        gate = jnp.dot(hn, wg_ref[:, f:f + FF_CHUNK], preferred_element_type=F32)
        up = jnp.dot(hn, wu_ref[:, f:f + FF_CHUNK], preferred_element_type=F32)
        a_scr[:, f:f + FF_CHUNK] = (gate * jax.nn.sigmoid(gate) * up).astype(BF16)
    y = jnp.dot(a_scr[...], wd_ref[...], preferred_element_type=F32)
    out_ref[...] = x + _rms(y, g_ref[3:4, :])


def _ffn(x, g, wg, wu, wd, attn=None, wo=None):
    b, s, c = x.shape
    d_ff = wg.shape[1]
    assert d_ff % FF_CHUNK == 0 and s % ROW_TILE == 0
    tm = ROW_TILE
    row = lambda i, j: (i, j, 0)
    with_attn = attn is not None
    in_specs = [pl.BlockSpec((None, tm, c), row)]
    args = [x]
    if with_attn:
        in_specs += [pl.BlockSpec((None, tm, c), row), _const_spec(wo.shape)]
        args += [attn, wo]
    in_specs += [_const_spec(g.shape), _const_spec(wg.shape), _const_spec(wu.shape),
                 _const_spec(wd.shape)]
    args += [g, wg, wu, wd]
    kern = functools.partial(_ffn_kernel, with_attn=with_attn, d_ff=d_ff)
    return pl.pallas_call(
        kern,
        grid=(b, s // tm),
        in_specs=in_specs,
        out_specs=pl.BlockSpec((None, tm, c), row),
        out_shape=jax.ShapeDtypeStruct(x.shape, F32),
        scratch_shapes=[pltpu.VMEM((tm, d_ff), BF16)],
        compiler_params=_params(2),
        name="ffn_attn" if with_attn else "ffn",
    )(*args)


def _mla_proj_kernel(x_ref, g_ref, wdn_ref, qn_ref, kvn_ref, wuqt_ref, wuk_ref, wuvt_ref,
                     cosf_ref, sinf_ref, cost_ref, sint_ref, k_out, q_out, v_out):
    tm = x_ref.shape[0]
    xn = _rms(x_ref[...], g_ref[0:1, :]).astype(BF16)
    down = jnp.dot(xn, wdn_ref[...], preferred_element_type=F32)
    c_q = down[:, :Q_LORA]
    c_kv = down[:, Q_LORA:Q_LORA + KV_LORA]
    kr = down[:, Q_LORA + KV_LORA:Q_LORA + KV_LORA + ROPE]
    kr_swapped = down[:, Q_LORA + KV_LORA + ROPE:]
    k_rope = kr * cosf_ref[...] + kr_swapped * sinf_ref[...]
    k_tail = jnp.concatenate([k_rope, jnp.zeros_like(k_rope)], axis=1).astype(BF16)
    cqn = _rms(c_q, qn_ref[...]).astype(BF16)
    ckvn = _rms(c_kv, kvn_ref[...]).astype(BF16)

    k_nope = jnp.dot(ckvn, wuk_ref[...], preferred_element_type=F32)
    for h in range(N_HEADS):
        k_out[h, :, :NOPE] = k_nope[:, h * NOPE:(h + 1) * NOPE].astype(BF16)
        k_out[h, :, NOPE:] = k_tail

    qt = lax.dot_general(wuqt_ref[...], cqn, NT_DIMS, preferred_element_type=F32) * Q_SCALE
    cos_t = cost_ref[...]
    sin_t = sint_ref[...]
    half = ROPE // 2
    hd = NOPE + ROPE
    for h in range(N_HEADS):
        base = h * hd
        q_out[h, :NOPE, :] = qt[base:base + NOPE].astype(BF16)
        x1 = qt[base + NOPE:base + NOPE + half]
        x2 = qt[base + NOPE + half:base + hd]
        q_out[h, NOPE:NOPE + half, :] = (x1 * cos_t - x2 * sin_t).astype(BF16)
        q_out[h, NOPE + half:hd, :] = (x2 * cos_t + x1 * sin_t).astype(BF16)
        q_out[h, hd:, :] = jnp.zeros((QK_PAD - hd, tm), BF16)

    vt = lax.dot_general(wuvt_ref[...], ckvn, NT_DIMS, preferred_element_type=F32)
    ones_rows = (lax.broadcasted_iota(jnp.int32, (V_ROWS - VDIM, tm), 0) == 0).astype(BF16)
    for h in range(N_HEADS):
        v_out[h, 0, :VDIM, :] = vt[h * VDIM:(h + 1) * VDIM].astype(BF16)
        v_out[h, 0, VDIM:, :] = ones_rows


def _mla_proj(x, g, wdn, qn, kvn, wuqt, wuk, wuvt, cosf, sinf, cos_t, sin_t):
    b, s, c = x.shape
    tm = KV_TILE
    assert s % tm == 0
    return pl.pallas_call(
        _mla_proj_kernel,
        grid=(b, s // tm),
        in_specs=[
            pl.BlockSpec((None, tm, c), lambda i, j: (i, j, 0)),
            _const_spec(g.shape), _const_spec(wdn.shape), _const_spec(qn.shape),
            _const_spec(kvn.shape), _const_spec(wuqt.shape), _const_spec(wuk.shape),
            _const_spec(wuvt.shape),
            pl.BlockSpec((tm, ROPE), lambda i, j: (j, 0)),
            pl.BlockSpec((tm, ROPE), lambda i, j: (j, 0)),
            pl.BlockSpec((ROPE // 2, tm), lambda i, j: (0, j)),
            pl.BlockSpec((ROPE // 2, tm), lambda i, j: (0, j)),
        ],
        out_specs=[
            pl.BlockSpec((None, N_HEADS, tm, QK_PAD), lambda i, j: (i, 0, j, 0)),
            pl.BlockSpec((None, N_HEADS, QK_PAD, tm), lambda i, j: (i, 0, 0, j)),
            pl.BlockSpec((None, N_HEADS, 1, V_ROWS, tm), lambda i, j: (i, 0, j, 0, 0)),
        ],
        out_shape=[
            jax.ShapeDtypeStruct((b, N_HEADS, s, QK_PAD), BF16),
            jax.ShapeDtypeStruct((b, N_HEADS, QK_PAD, s), BF16),
            jax.ShapeDtypeStruct((b, N_HEADS, s // tm, V_ROWS, tm), BF16),
        ],
        compiler_params=_params(2),
        name="mla_proj",
    )(x, g, wdn, qn, kvn, wuqt, wuk, wuvt, cosf, sinf, cos_t, sin_t)


def _attn_kernel(q_ref, qnext_ref, k_ref, v_ref, o_ref, acc_ref, s_ref, *, nk, tk, nb):
    qt = q_ref[...]
    tq = qt.shape[1]
    acc_ref[...] = jnp.zeros_like(acc_ref)

    def scores(qmat, i, slot):
        start = i * tk if isinstance(i, int) else pl.multiple_of(i * tk, tk)
        k = k_ref[pl.ds(start, tk), :]
        s_ref[slot] = jnp.dot(k, qmat, preferred_element_type=F32)

    def consume(i, slot, m):
        s = s_ref[slot]
        m_new = jnp.maximum(m, jnp.max(s, axis=0, keepdims=True))
        alpha = jnp.exp2(m - m_new)
        p = jnp.exp2(s - m_new).astype(BF16)
        pv = jnp.dot(v_ref[i], p, preferred_element_type=F32)
        acc_ref[...] = alpha * acc_ref[...] + pv
        return m_new

    def super_block(base, m, last):
        for t in range(2 * nb):
            if last and t >= nb:
                scores(qnext_ref[...], t - nb, t - nb)
            else:
                scores(qt, base + t + nb, (t + nb) % (2 * nb))
            m = consume(base + t, t, m)
        return m

    @pl.when(pl.program_id(2) == 0)
    def _():
        for u in range(nb):
            scores(qt, u, u)

    n_super = nk // (2 * nb)
    m = jnp.full((1, tq), -jnp.inf, F32)
    m = lax.fori_loop(
        0, n_super - 1, lambda j, c: super_block(j * (2 * nb), c, False), m)
    super_block((n_super - 1) * (2 * nb), m, True)
    out =acc_ref[:VDIM, :] / acc_ref[VDIM:VDIM + 1, :]
    o_ref[...] = out.T.astype(BF16)


def _attention(qt, kc, vt):
    b, h, s, _ = kc.shape
    tq, tk = Q_TILE, KV_TILE
    nk = s // tk
    nb = SCORE_LOOKAHEAD
    assert nk % (2 * nb) == 0
    kern = functools.partial(_attn_kernel, nk=nk, tk=tk, nb=nb)
    nq = s // tq
    return pl.pallas_call(
        kern,
        grid=(b, h, nq),
        in_specs=[
            pl.BlockSpec((None, None, QK_PAD, tq), lambda i, j, q: (i, j, 0, q)),
            pl.BlockSpec((None, None, QK_PAD, tq),
                         lambda i, j, q: (i, j, 0, jnp.minimum(q + 1, nq - 1))),
            pl.BlockSpec((None, None, s, QK_PAD), lambda i, j, q: (i, j, 0, 0)),
            pl.BlockSpec((None, None, nk, V_ROWS, tk), lambda i, j, q: (i, j, 0, 0, 0)),
        ],
        out_specs=pl.BlockSpec((None, tq, VDIM), lambda i, j, q: (i, q, j)),
        out_shape=jax.ShapeDtypeStruct((b, s, h * VDIM), BF16),
        scratch_shapes=[pltpu.VMEM((V_ROWS, tq), F32), pltpu.VMEM((2 * nb, tk, tq), F32)],
        compiler_params=pltpu.CompilerParams(
            dimension_semantics=("parallel", "parallel", "arbitrary"),
            vmem_limit_bytes=V7X_VMEM_LIMIT_BYTES),
        name="attention",
    )(qt, qt, kc, vt)


def _rope_tables(s):
    inv_freq = 1.0 / (ROPE_THETA ** (jnp.arange(0, ROPE, 2, dtype=F32) / ROPE))
    ang = jnp.arange(s, dtype=F32)[:, None] * inv_freq[None, :]
    cos, sin = jnp.cos(ang), jnp.sin(ang)
    cosf = jnp.concatenate([cos, cos], axis=1)
    sinf = jnp.concatenate([-sin, sin], axis=1)
    return cosf, sinf, cos.T, sin.T


def _prep_weights(fnet_w_out, mla_w_down, mla_w_uq, mla_w_ukv, mla_w_o,
                  ffn_w_gate, ffn_w_up, ffn_w_down):
    w = {}
    w["wf"] = _fold_channel_dft(fnet_w_out[0])
    wd = mla_w_down[0]
    r0 = Q_LORA + KV_LORA
    half = ROPE // 2
    w["wdn"] = jnp.concatenate([wd, wd[:, r0 + half:r0 + ROPE], wd[:, r0:r0 + half]],
                               axis=1).astype(BF16)
    w["wuqt"] = mla_w_uq[0].T.astype(BF16)
    wukv = mla_w_ukv[0].reshape(KV_LORA, N_HEADS, NOPE + VDIM)
    w["wuk"] = wukv[:, :, :NOPE].reshape(KV_LORA, N_HEADS * NOPE).astype(BF16)
    w["wuvt"] = wukv[:, :, NOPE:].reshape(KV_LORA, N_HEADS * VDIM).T.astype(BF16)
    w["wo"] = mla_w_o[0].astype(BF16)
    w["wg"] = ffn_w_gate.astype(BF16)
    w["wu"] = ffn_w_up.astype(BF16)
    w["wd"] = ffn_w_down.astype(BF16)
    return w


def _trunk(x, norm_g, fnet_b_out, mla_q_norm, mla_kv_norm, w):
    b, s, c = x.shape
    n1, n2 = _split_seq(s)
    kc = FOURIER_ROWS // n1
    g0, g1 = norm_g[0], norm_g[1]
    tp = _fourier_a(x, g0, n1, n2, kc)
    x = _fourier_b(tp, x, w["wf"], fnet_b_out[0][None, :], g0, n1, n2, kc)
    x = _ffn(x, g0, w["wg"][0], w["wu"][0], w["wd"][0])
    cosf, sinf, cos_t, sin_t = _rope_tables(s)
    kc_, qt, vt = _mla_proj(x, g1, w["wdn"], mla_q_norm[0][None, :], mla_kv_norm[0][None, :],
                            w["wuqt"], w["wuk"], w["wuvt"], cosf, sinf, cos_t, sin_t)
    o = _attention(qt, kc_, vt)
    return _ffn(x, g1, w["wg"][1], w["wu"][1], w["wd"][1], attn=o, wo=w["wo"])


def kernel(x_prompt, x_sample, norm_g, fnet_w_out, fnet_b_out, mla_w_down, mla_q_norm, mla_w_uq,
           mla_kv_norm, mla_w_ukv, mla_w_o, ffn_w_gate, ffn_w_up, ffn_w_down):
    w = _prep_weights(fnet_w_out, mla_w_down, mla_w_uq, mla_w_ukv, mla_w_o,
                      ffn_w_gate, ffn_w_up, ffn_w_down)
    y_prompt = _trunk(x_prompt, norm_g, fnet_b_out, mla_q_norm, mla_kv_norm, w)
    y_sample = _trunk(x_sample, norm_g, fnet_b_out, mla_q_norm, mla_kv_norm, w)
    return (y_prompt, y_sample)
```

```python
import functools
import math

import numpy as np
import jax
import jax.numpy as jnp
from jax import lax
from jax.experimental import pallas as pl
from jax.experimental.pallas import tpu as pltpu

F32 = jnp.float32
BF16 = jnp.bfloat16

N_GROUPS = 4
N_HEADS = 8
NOPE = 128
ROPE = 64
VDIM = 128
V_ROWS = VDIM + 16
Q_LORA = 256
KV_LORA = 256
QK_PAD = 256
ROPE_THETA = 10000.0
NORM_EPS = 1e-6
Q_SCALE = math.log2(math.e) / math.sqrt(NOPE + ROPE)

V7X_VMEM_LIMIT_BYTES = 56 * 1024 * 1024
ROW_TILE = 512
Q_TILE = 512
KV_TILE = 512
SCORE_LOOKAHEAD = 4
FF_CHUNK = 256
LANES = 128
SUBLANES = 8

NT_DIMS = (((1,), (1,)), ((), ()))


def _rms(x, g):
    ms = jnp.mean(x * x, axis=-1, keepdims=True)
    return x * lax.rsqrt(ms + NORM_EPS) * g


def _const_spec(shape):
    return pl.BlockSpec(shape, lambda *_: (0,) * len(shape), pipeline_mode=pl.Buffered(1))


def _params(n_axes):
    return pltpu.CompilerParams(dimension_semantics=("parallel",) * n_axes,
                                vmem_limit_bytes=V7X_VMEM_LIMIT_BYTES)


def _split_seq(s):
    lg = int(round(math.log2(s)))
    assert 2 ** lg == s
    n1 = 2 ** ((lg + 1) // 2)
    return n1, s // n1


def _dft_tables(n1, n2):
    n = n1 * n2
    k2 = np.arange(n2)
    a2 = 2.0 * np.pi * np.outer(k2, k2) / n2
    f2 = np.concatenate([np.cos(a2), -np.sin(a2)], axis=0)
    k1 = np.arange(n1)
    a1 = 2.0 * np.pi * np.outer(k1, k1) / n1
    c1, s1 = np.cos(a1), np.sin(a1)
    g1 = np.block([[c1, s1], [-s1, c1]])
    at = 2.0 * np.pi * np.outer(k2, np.arange(n1)) / n
    return (f2.astype(np.float32), g1.astype(np.float32),
            np.cos(at).astype(np.float32), np.sin(at).astype(np.float32))


def _flatten_slabs(lane_refs, flat_ref):
    for t, r in enumerate(lane_refs):
        flat_ref[t] = r[...].reshape(r.shape[0] * r.shape[1], LANES)


def _gather_rows(flat_ref, start, count, stride):
    return jnp.concatenate(
        [flat_ref[t, pl.ds(start, count, stride=stride), :] for t in range(flat_ref.shape[0])],
        axis=1)


def _lane_specs(rows, kc, n_lane_tiles):
    return [pl.BlockSpec((None, rows, kc, LANES),
                         functools.partial(lambda i, j, t: (i, 0, j, t), t=t))
            for t in range(n_lane_tiles)]


def _pack_pair(re, im):
    hi = lax.bitcast_convert_type(re.astype(BF16).astype(F32), jnp.uint32)
    lo = lax.bitcast_convert_type(im.astype(BF16).astype(F32), jnp.uint32)
    return hi | (lo >> 16)


def _unpack_pair(w):
    re = lax.bitcast_convert_type(w & jnp.uint32(0xFFFF0000), F32)
    im = lax.bitcast_convert_type(w << 16, F32)
    return re.astype(BF16), im.astype(BF16)


def _fourier_a_kernel(*refs, n2, kc, c):
    n_lt = c // LANES
    g_ref, f2_ref, twc_ref, tws_ref, out_ref, flat_ref = refs[n_lt:]
    _flatten_slabs(refs[:n_lt], flat_ref)
    f2 = f2_ref[...]
    g = g_ref[0:1, :]
    for jj in range(kc):
        xs = _gather_rows(flat_ref, jj, n2, kc)
        xn = _rms(xs, g).astype(BF16)
        t = jnp.dot(f2, xn, preferred_element_type=F32)
        tr, ti = t[:n2], t[n2:]
        cw = twc_ref[:, jj:jj + 1]
        sw = tws_ref[:, jj:jj + 1]
        out_ref[jj] = _pack_pair(tr * cw + ti * sw, ti * cw - tr * sw)


def _fourier_a(x, g, n1, n2, kc):
    b, s, c = x.shape
    f2, _, twc, tws = _dft_tables(n1, n2)
    f2 = jnp.asarray(f2).astype(BF16)
    twc = jnp.asarray(twc).reshape(n2, n1 // kc, kc).transpose(1, 0, 2)
    tws = jnp.asarray(tws).reshape(n2, n1 // kc, kc).transpose(1, 0, 2)
    xv = x.reshape(b, n2, n1, c)
    n_lt = c // LANES
    kern = functools.partial(_fourier_a_kernel, n2=n2, kc=kc, c=c)
    return pl.pallas_call(
        kern,
        grid=(b, n1 // kc),
        in_specs=_lane_specs(n2, kc, n_lt) + [
            _const_spec(g.shape),
            _const_spec(f2.shape),
            pl.BlockSpec((None, n2, kc), lambda i, j: (j, 0, 0)),
            pl.BlockSpec((None, n2, kc), lambda i, j: (j, 0, 0)),
        ],
        out_specs=pl.BlockSpec((None, kc, n2, c), lambda i, j: (i, j, 0, 0)),
        out_shape=jax.ShapeDtypeStruct((b, n1, n2, c), jnp.uint32),
        scratch_shapes=[pltpu.VMEM((n_lt, n2 * kc, LANES), F32)],
        compiler_params=_params(2),
        name="fourier_a",
    )(*([xv] * n_lt), g, f2, twc, tws)


def _fourier_b_kernel(*refs, n1, kc, c):
    n_lt = c // LANES
    x_ref, g1_ref, wf_ref, b_ref, g_ref, out_ref, u_scr, y_scr, flat_ref = refs[n_lt:]
    _flatten_slabs(refs[:n_lt], flat_ref)
    g1 = g1_ref[...]
    for t in range(kc):
        tr, ti = _unpack_pair(_gather_rows(flat_ref, t, n1, kc))
        u = (jnp.dot(g1[:, :n1], tr, preferred_element_type=F32)
             + jnp.dot(g1[:, n1:], ti, preferred_element_type=F32))
        u_scr[t * n1:(t + 1) * n1, :c] = u[:n1].astype(BF16)
        u_scr[t * n1:(t + 1) * n1, c:] = u[n1:].astype(BF16)
    m = jnp.dot(u_scr[...], wf_ref[...], preferred_element_type=F32) + b_ref[...]
    y = _rms(m, g_ref[1:2, :])
    pitch = y_scr.shape[1] // kc
    for lt in range(n_lt):
        for t in range(kc):
            y_scr[lt, t * pitch:t * pitch + n1, :] = y[t * n1:(t + 1) * n1,
                                                       lt * LANES:(lt + 1) * LANES]
    for k1 in range(n1):
        rows = jnp.concatenate(
            [y_scr[lt, pl.ds(k1, kc, stride=pitch), :] for lt in range(n_lt)], axis=1)
        out_ref[k1] = x_ref[k1] + rows


def _fourier_b(tp, x, wf, bias, g, n1, n2, kc):
    b, s, c = x.shape
    _, g1, _, _ = _dft_tables(n1, n2)
    g1 = jnp.asarray(g1).astype(BF16)
    xv = x.reshape(b, n1, n2, c)
    n_lt = c // LANES
    pitch = n1 + SUBLANES
    kern = functools.partial(_fourier_b_kernel, n1=n1, kc=kc, c=c)
    out = pl.pallas_call(
        kern,
        grid=(b, n2 // kc),
        in_specs=_lane_specs(n1, kc, n_lt) + [
            pl.BlockSpec((None, n1, kc, c), lambda i, j: (i, 0, j, 0)),
            _const_spec(g1.shape),
            _const_spec(wf.shape),
            _const_spec(bias.shape),
            _const_spec(g.shape),
        ],
        out_specs=pl.BlockSpec((None, n1, kc, c), lambda i, j: (i, 0, j, 0)),
        out_shape=jax.ShapeDtypeStruct(xv.shape, F32),
        scratch_shapes=[pltpu.VMEM((kc * n1, 2 * c), BF16),
                        pltpu.VMEM((n_lt, kc * pitch, LANES), F32),
                        pltpu.VMEM((n_lt, n1 * kc, LANES), jnp.uint32)],
        compiler_params=_params(2),
        name="fourier_b",
    )(*([tp] * n_lt), xv, g1, wf, bias, g)
    return out.reshape(b, s, c)


def _fold_kernel(cc_ref, sc_ref, w_ref, out_ref):
    w = w_ref[...]
    out_ref[0] = jnp.dot(cc_ref[...], w, preferred_element_type=F32,
                         precision=lax.Precision.HIGHEST).astype(BF16)
    out_ref[1] = jnp.dot(sc_ref[...], w, preferred_element_type=F32,
                         precision=lax.Precision.HIGHEST).astype(BF16)


def _fold_channel_dft(w_out):
    d = w_out.shape[0]
    gd = d // N_GROUPS
    idx = np.arange(gd)
    ang = 2.0 * np.pi * np.outer(idx, idx) / gd
    cc = jnp.asarray(np.cos(ang).astype(np.float32))
    sc = jnp.asarray(np.sin(ang).astype(np.float32))
    out = pl.pallas_call(
        _fold_kernel,
        grid=(N_GROUPS,),
        in_specs=[
            pl.BlockSpec((gd, gd), lambda i: (0, 0)),
            pl.BlockSpec((gd, gd), lambda i: (0, 0)),
            pl.BlockSpec((gd, d), lambda i: (i, 0)),
        ],
        out_specs=pl.BlockSpec((2, gd, d), lambda i: (0, i, 0)),
        out_shape=jax.ShapeDtypeStruct((2, d, d), BF16),
        compiler_params=_params(1),
        name="fold_channel_dft",
    )(cc, sc, w_out)
    return out.reshape(2 * d, d)


def _ffn_kernel(*refs, with_attn, d_ff):
    if with_attn:
        x_ref, o_ref, wo_ref, g_ref, wg_ref, wu_ref, wd_ref, out_ref, a_scr = refs
    else:
        x_ref, g_ref, wg_ref, wu_ref, wd_ref, out_ref, a_scr = refs
    x = x_ref[...]
    if with_attn:
        m = jnp.dot(o_ref[...], wo_ref[...], preferred_element_type=F32)
        x = x + _rms(m, g_ref[1:2, :])
    hn = _rms(x, g_ref[2:3, :]).astype(BF16)
    for f in range(0, d_ff, FF_CHUNK):
        gate = jnp.dot(hn, wg_ref[:, f:f + FF_CHUNK], preferred_element_type=F32)
        up = jnp.dot(hn, wu_ref[:, f:f + FF_CHUNK], preferred_element_type=F32)
        a_scr[:, f:f + FF_CHUNK] = (gate * jax.nn.sigmoid(gate) * up).astype(BF16)
    y = jnp.dot(a_scr[...], wd_ref[...], preferred_element_type=F32)
    out_ref[...] = x + _rms(y, g_ref[3:4, :])


def _ffn(x, g, wg, wu, wd, attn=None, wo=None):
    b, s, c = x.shape
    d_ff = wg.shape[1]
    assert d_ff % FF_CHUNK == 0 and s % ROW_TILE == 0
    tm = ROW_TILE
    row = lambda i, j: (i, j, 0)
    with_attn = attn is not None
    in_specs = [pl.BlockSpec((None, tm, c), row)]
    args = [x]
    if with_attn:
        in_specs += [pl.BlockSpec((None, tm, c), row), _const_spec(wo.shape)]
        args += [attn, wo]
    in_specs += [_const_spec(g.shape), _const_spec(wg.shape), _const_spec(wu.shape),
                 _const_spec(wd.shape)]
    args += [g, wg, wu, wd]
    kern = functools.partial(_ffn_kernel, with_attn=with_attn, d_ff=d_ff)
    return pl.pallas_call(
        kern,
        grid=(b, s // tm),
        in_specs=in_specs,
        out_specs=pl.BlockSpec((None, tm, c), row),
        out_shape=jax.ShapeDtypeStruct(x.shape, F32),
        scratch_shapes=[pltpu.VMEM((tm, d_ff), BF16)],
        compiler_params=_params(2),
        name="ffn_attn" if with_attn else "ffn",
    )(*args)


def _mla_proj_kernel(x_ref, g_ref, wdn_ref, qn_ref, kvn_ref, wuqt_ref, wuk_ref, wuvt_ref,
                     cosf_ref, sinf_ref, cost_ref, sint_ref, k_out, q_out, v_out):
    tm = x_ref.shape[0]
    xn = _rms(x_ref[...], g_ref[0:1, :]).astype(BF16)
    down = jnp.dot(xn, wdn_ref[...], preferred_element_type=F32)
    c_q = down[:, :Q_LORA]
    c_kv = down[:, Q_LORA:Q_LORA + KV_LORA]
    kr = down[:, Q_LORA + KV_LORA:Q_LORA + KV_LORA + ROPE]
    kr_swapped = down[:, Q_LORA + KV_LORA + ROPE:]
    k_rope = kr * cosf_ref[...] + kr_swapped * sinf_ref[...]
    k_tail = jnp.concatenate([k_rope, jnp.zeros_like(k_rope)], axis=1).astype(BF16)
    cqn = _rms(c_q, qn_ref[...]).astype(BF16)
    ckvn = _rms(c_kv, kvn_ref[...]).astype(BF16)

    k_nope = jnp.dot(ckvn, wuk_ref[...], preferred_element_type=F32)
    for h in range(N_HEADS):
        k_out[h, :, :NOPE] = k_nope[:, h * NOPE:(h + 1) * NOPE].astype(BF16)
        k_out[h, :, NOPE:] = k_tail

    qt = lax.dot_general(wuqt_ref[...], cqn, NT_DIMS, preferred_element_type=F32) * Q_SCALE
    cos_t = cost_ref[...]
    sin_t = sint_ref[...]
    half = ROPE // 2
    hd = NOPE + ROPE
    for h in range(N_HEADS):
        base = h * hd
        q_out[h, :NOPE, :] = qt[base:base + NOPE].astype(BF16)
        x1 = qt[base + NOPE:base + NOPE + half]
        x2 = qt[base + NOPE + half:base + hd]
        q_out[h, NOPE:NOPE + half, :] = (x1 * cos_t - x2 * sin_t).astype(BF16)
        q_out[h, NOPE + half:hd, :] = (x2 * cos_t + x1 * sin_t).astype(BF16)
        q_out[h, hd:, :] = jnp.zeros((QK_PAD - hd, tm), BF16)

    vt = lax.dot_general(wuvt_ref[...], ckvn, NT_DIMS, preferred_element_type=F32)
    ones_rows = (lax.broadcasted_iota(jnp.int32, (V_ROWS - VDIM, tm), 0) == 0).astype(BF16)
    for h in range(N_HEADS):
        v_out[h, 0, :VDIM, :] = vt[h * VDIM:(h + 1) * VDIM].astype(BF16)
        v_out[h, 0, VDIM:, :] = ones_rows


def _mla_proj(x, g, wdn, qn, kvn, wuqt, wuk, wuvt, cosf, sinf, cos_t, sin_t):
    b, s, c = x.shape
    tm = KV_TILE
    assert s % tm == 0
    return pl.pallas_call(
        _mla_proj_kernel,
        grid=(b, s // tm),
        in_specs=[
            pl.BlockSpec((None, tm, c), lambda i, j: (i, j, 0)),
            _const_spec(g.shape), _const_spec(wdn.shape), _const_spec(qn.shape),
            _const_spec(kvn.shape), _const_spec(wuqt.shape), _const_spec(wuk.shape),
            _const_spec(wuvt.shape),
            pl.BlockSpec((tm, ROPE), lambda i, j: (j, 0)),
            pl.BlockSpec((tm, ROPE), lambda i, j: (j, 0)),
            pl.BlockSpec((ROPE // 2, tm), lambda i, j: (0, j)),
            pl.BlockSpec((ROPE // 2, tm), lambda i, j: (0, j)),
        ],
        out_specs=[
            pl.BlockSpec((None, N_HEADS, tm, QK_PAD), lambda i, j: (i, 0, j, 0)),
            pl.BlockSpec((None, N_HEADS, QK_PAD, tm), lambda i, j: (i, 0, 0, j)),
            pl.BlockSpec((None, N_HEADS, 1, V_ROWS, tm), lambda i, j: (i, 0, j, 0, 0)),
        ],
        out_shape=[
            jax.ShapeDtypeStruct((b, N_HEADS, s, QK_PAD), BF16),
            jax.ShapeDtypeStruct((b, N_HEADS, QK_PAD, s), BF16),
            jax.ShapeDtypeStruct((b, N_HEADS, s // tm, V_ROWS, tm), BF16),
        ],
        compiler_params=_params(2),
        name="mla_proj",
    )(x, g, wdn, qn, kvn, wuqt, wuk, wuvt, cosf, sinf, cos_t, sin_t)


def _attn_kernel(q_ref, qnext_ref, k_ref, v_ref, o_ref, acc_ref, s_ref, *, nk, tk, nb):
    qt = q_ref[...]
    tq = qt.shape[1]
    acc_ref[...] = jnp.zeros_like(acc_ref)

    def scores(qmat, i, slot):
        start = i * tk if isinstance(i, int) else pl.multiple_of(i * tk, tk)
        k = k_ref[pl.ds(start, tk), :]
        s_ref[slot] = jnp.dot(k, qmat, preferred_element_type=F32)

    def consume(i, slot, m):
        s = s_ref[slot]
        m_new = jnp.maximum(m, jnp.max(s, axis=0, keepdims=True))
        alpha = jnp.exp2(m - m_new)
        p = jnp.exp2(s - m_new).astype(BF16)
        pv = jnp.dot(v_ref[i], p, preferred_element_type=F32)
        acc_ref[...] = alpha * acc_ref[...] + pv
        return m_new

    def super_block(base, m, last):
        for t in range(2 * nb):
            if last and t >= nb:
                scores(qnext_ref[...], t - nb, t - nb)
            else:
                scores(qt, base + t + nb, (t + nb) % (2 * nb))
            m = consume(base + t, t, m)
        return m

    @pl.when(pl.program_id(2) == 0)
    def _():
        for u in range(nb):
            scores(qt, u, u)

    n_super = nk // (2 * nb)
    m = jnp.full((1, tq), -jnp.inf, F32)
    m = lax.fori_loop(
        0, n_super - 1, lambda j, c: super_block(j * (2 * nb), c, False), m)
    super_block((n_super - 1) * (2 * nb), m, True)
    out =acc_ref[:VDIM, :] / acc_ref[VDIM:VDIM + 1, :]
    o_ref[...] = out.T.astype(BF16)


def _attention(qt, kc, vt):
    b, h, s, _ = kc.shape
    tq, tk = Q_TILE, KV_TILE
    nk = s // tk
    nb = SCORE_LOOKAHEAD
    assert nk % (2 * nb) == 0
    kern = functools.partial(_attn_kernel, nk=nk, tk=tk, nb=nb)
    nq = s // tq
    return pl.pallas_call(
        kern,
        grid=(b, h, nq),
        in_specs=[
            pl.BlockSpec((None, None, QK_PAD, tq), lambda i, j, q: (i, j, 0, q)),
            pl.BlockSpec((None, None, QK_PAD, tq),
                         lambda i, j, q: (i, j, 0, jnp.minimum(q + 1, nq - 1))),
            pl.BlockSpec((None, None, s, QK_PAD), lambda i, j, q: (i, j, 0, 0)),
            pl.BlockSpec((None, None, nk, V_ROWS, tk), lambda i, j, q: (i, j, 0, 0, 0)),
        ],
        out_specs=pl.BlockSpec((None, tq, VDIM), lambda i, j, q: (i, q, j)),
        out_shape=jax.ShapeDtypeStruct((b, s, h * VDIM), BF16),
        scratch_shapes=[pltpu.VMEM((V_ROWS, tq), F32), pltpu.VMEM((2 * nb, tk, tq), F32)],
        compiler_params=pltpu.CompilerParams(
            dimension_semantics=("parallel", "parallel", "arbitrary"),
            vmem_limit_bytes=V7X_VMEM_LIMIT_BYTES),
        name="attention",
    )(qt, qt, kc, vt)


def _rope_tables(s):
    inv_freq = 1.0 / (ROPE_THETA ** (jnp.arange(0, ROPE, 2, dtype=F32) / ROPE))
    ang = jnp.arange(s, dtype=F32)[:, None] * inv_freq[None, :]
    cos, sin = jnp.cos(ang), jnp.sin(ang)
    cosf = jnp.concatenate([cos, cos], axis=1)
    sinf = jnp.concatenate([-sin, sin], axis=1)
    return cosf, sinf, cos.T, sin.T


def _prep_weights(fnet_w_out, mla_w_down, mla_w_uq, mla_w_ukv, mla_w_o,
                  ffn_w_gate, ffn_w_up, ffn_w_down):
    w = {}
    w["wf"] = _fold_channel_dft(fnet_w_out[0])
    wd = mla_w_down[0]
    r0 = Q_LORA + KV_LORA
    half = ROPE // 2
    w["wdn"] = jnp.concatenate([wd, wd[:, r0 + half:r0 + ROPE], wd[:, r0:r0 + half]],
                               axis=1).astype(BF16)
    w["wuqt"] = mla_w_uq[0].T.astype(BF16)
    wukv = mla_w_ukv[0].reshape(KV_LORA, N_HEADS, NOPE + VDIM)
    w["wuk"] = wukv[:, :, :NOPE].reshape(KV_LORA, N_HEADS * NOPE).astype(BF16)
    w["wuvt"] = wukv[:, :, NOPE:].reshape(KV_LORA, N_HEADS * VDIM).T.astype(BF16)
    w["wo"] = mla_w_o[0].astype(BF16)
    w["wg"] = ffn_w_gate.astype(BF16)
    w["wu"] = ffn_w_up.astype(BF16)
    w["wd"] = ffn_w_down.astype(BF16)
    return w


def _trunk(x, norm_g, fnet_b_out, mla_q_norm, mla_kv_norm, w):
    b, s, c = x.shape
    n1, n2 = _split_seq(s)
    kc = SUBLANES
    g0, g1 = norm_g[0], norm_g[1]
    tp = _fourier_a(x, g0, n1, n2, kc)
    x = _fourier_b(tp, x, w["wf"], fnet_b_out[0][None, :], g0, n1, n2, kc)
    x = _ffn(x, g0, w["wg"][0], w["wu"][0], w["wd"][0])
    cosf, sinf, cos_t, sin_t = _rope_tables(s)
    kc_, qt, vt = _mla_proj(x, g1, w["wdn"], mla_q_norm[0][None, :], mla_kv_norm[0][None, :],
                            w["wuqt"], w["wuk"], w["wuvt"], cosf, sinf, cos_t, sin_t)
    o = _attention(qt, kc_, vt)
    return _ffn(x, g1, w["wg"][1], w["wu"][1], w["wd"][1], attn=o, wo=w["wo"])


def kernel(x_prompt, x_sample, norm_g, fnet_w_out, fnet_b_out, mla_w_down, mla_q_norm, mla_w_uq,
           mla_kv_norm, mla_w_ukv, mla_w_o, ffn_w_gate, ffn_w_up, ffn_w_down):
    w = _prep_weights(fnet_w_out, mla_w_down, mla_w_uq, mla_w_ukv, mla_w_o,
                      ffn_w_gate, ffn_w_up, ffn_w_down)
    y_prompt = _trunk(x_prompt, norm_g, fnet_b_out, mla_q_norm, mla_kv_norm, w)
    y_sample = _trunk(x_sample, norm_g, fnet_b_out, mla_q_norm, mla_kv_norm, w)
    return (y_prompt, y_sample)
```

```python
import functools
import math

import numpy as np
import jax
import jax.numpy as jnp
from jax import lax
from jax.experimental import pallas as pl
from jax.experimental.pallas import tpu as pltpu

F32 = jnp.float32
BF16 = jnp.bfloat16

N_GROUPS = 4
N_HEADS = 8
NOPE = 128
ROPE = 64
VDIM = 128
V_ROWS = VDIM + 16
Q_LORA = 256
KV_LORA = 256
QK_PAD = 256
ROPE_THETA = 10000.0
NORM_EPS = 1e-6
Q_SCALE = math.log2(math.e) / math.sqrt(NOPE + ROPE)

V7X_VMEM_LIMIT_BYTES = 56 * 1024 * 1024
ROW_TILE = 512
Q_TILE = 1024
KV_TILE = 512
SCORE_LOOKAHEAD = 4
FF_CHUNK = 256
LANES = 128
SUBLANES = 8

NT_DIMS = (((1,), (1,)), ((), ()))


def _rms(x, g):
    ms = jnp.mean(x * x, axis=-1, keepdims=True)
    return x * lax.rsqrt(ms + NORM_EPS) * g


def _const_spec(shape):
    return pl.BlockSpec(shape, lambda *_: (0,) * len(shape), pipeline_mode=pl.Buffered(1))


def _params(n_axes):
    return pltpu.CompilerParams(dimension_semantics=("parallel",) * n_axes,
                                vmem_limit_bytes=V7X_VMEM_LIMIT_BYTES)


def _split_seq(s):
    lg = int(round(math.log2(s)))
    assert 2 ** lg == s
    n1 = 2 ** ((lg + 1) // 2)
    return n1, s // n1


def _dft_tables(n1, n2):
    n = n1 * n2
    k2 = np.arange(n2)
    a2 = 2.0 * np.pi * np.outer(k2, k2) / n2
    f2 = np.concatenate([np.cos(a2), -np.sin(a2)], axis=0)
    k1 = np.arange(n1)
    a1 = 2.0 * np.pi * np.outer(k1, k1) / n1
    c1, s1 = np.cos(a1), np.sin(a1)
    g1 = np.block([[c1, s1], [-s1, c1]])
    at = 2.0 * np.pi * np.outer(k2, np.arange(n1)) / n
    return (f2.astype(np.float32), g1.astype(np.float32),
            np.cos(at).astype(np.float32), np.sin(at).astype(np.float32))


def _flatten_slabs(lane_refs, flat_ref):
    for t, r in enumerate(lane_refs):
        flat_ref[t] = r[...].reshape(r.shape[0] * r.shape[1], LANES)


def _gather_rows(flat_ref, start, count, stride):
    return jnp.concatenate(
        [flat_ref[t, pl.ds(start, count, stride=stride), :] for t in range(flat_ref.shape[0])],
        axis=1)


def _lane_specs(rows, kc, n_lane_tiles):
    return [pl.BlockSpec((None, rows, kc, LANES),
                         functools.partial(lambda i, j, t: (i, 0, j, t), t=t))
            for t in range(n_lane_tiles)]


def _pack_pair(re, im):
    hi = lax.bitcast_convert_type(re.astype(BF16).astype(F32), jnp.uint32)
    lo = lax.bitcast_convert_type(im.astype(BF16).astype(F32), jnp.uint32)
    return hi | (lo >> 16)


def _unpack_pair(w):
    re = lax.bitcast_convert_type(w & jnp.uint32(0xFFFF0000), F32)
    im = lax.bitcast_convert_type(w << 16, F32)
    return re.astype(BF16), im.astype(BF16)


def _fourier_a_kernel(*refs, n2, kc, c):
    n_lt = c // LANES
    g_ref, f2_ref, twc_ref, tws_ref, out_ref, flat_ref = refs[n_lt:]
    _flatten_slabs(refs[:n_lt], flat_ref)
    f2 = f2_ref[...]
    g = g_ref[0:1, :]
    for jj in range(kc):
        xs = _gather_rows(flat_ref, jj, n2, kc)
        xn = _rms(xs, g).astype(BF16)
        t = jnp.dot(f2, xn, preferred_element_type=F32)
        tr, ti = t[:n2], t[n2:]
        cw = twc_ref[:, jj:jj + 1]
        sw = tws_ref[:, jj:jj + 1]
        out_ref[jj] = _pack_pair(tr * cw + ti * sw, ti * cw - tr * sw)


def _fourier_a(x, g, n1, n2, kc):
    b, s, c = x.shape
    f2, _, twc, tws = _dft_tables(n1, n2)
    f2 = jnp.asarray(f2).astype(BF16)
    twc = jnp.asarray(twc).reshape(n2, n1 // kc, kc).transpose(1, 0, 2)
    tws = jnp.asarray(tws).reshape(n2, n1 // kc, kc).transpose(1, 0, 2)
    xv = x.reshape(b, n2, n1, c)
    n_lt = c // LANES
    kern = functools.partial(_fourier_a_kernel, n2=n2, kc=kc, c=c)
    return pl.pallas_call(
        kern,
        grid=(b, n1 // kc),
        in_specs=_lane_specs(n2, kc, n_lt) + [
            _const_spec(g.shape),
            _const_spec(f2.shape),
            pl.BlockSpec((None, n2, kc), lambda i, j: (j, 0, 0)),
            pl.BlockSpec((None, n2, kc), lambda i, j: (j, 0, 0)),
        ],
        out_specs=pl.BlockSpec((None, kc, n2, c), lambda i, j: (i, j, 0, 0)),
        out_shape=jax.ShapeDtypeStruct((b, n1, n2, c), jnp.uint32),
        scratch_shapes=[pltpu.VMEM((n_lt, n2 * kc, LANES), F32)],
        compiler_params=_params(2),
        name="fourier_a",
    )(*([xv] * n_lt), g, f2, twc, tws)


def _fourier_b_kernel(*refs, n1, kc, c):
    n_lt = c // LANES
    x_ref, g1_ref, wf_ref, b_ref, g_ref, out_ref, u_scr, y_scr, flat_ref = refs[n_lt:]
    _flatten_slabs(refs[:n_lt], flat_ref)
    g1 = g1_ref[...]
    for t in range(kc):
        tr, ti = _unpack_pair(_gather_rows(flat_ref, t, n1, kc))
        u = (jnp.dot(g1[:, :n1], tr, preferred_element_type=F32)
             + jnp.dot(g1[:, n1:], ti, preferred_element_type=F32))
        u_scr[t * n1:(t + 1) * n1, :c] = u[:n1].astype(BF16)
        u_scr[t * n1:(t + 1) * n1, c:] = u[n1:].astype(BF16)
    m = jnp.dot(u_scr[...], wf_ref[...], preferred_element_type=F32) + b_ref[...]
    y = _rms(m, g_ref[1:2, :])
    pitch = y_scr.shape[1] // kc
    for lt in range(n_lt):
        for t in range(kc):
            y_scr[lt, t * pitch:t * pitch + n1, :] = y[t * n1:(t + 1) * n1,
                                                       lt * LANES:(lt + 1) * LANES]
    for k1 in range(n1):
        rows = jnp.concatenate(
            [y_scr[lt, pl.ds(k1, kc, stride=pitch), :] for lt in range(n_lt)], axis=1)
        out_ref[k1] = x_ref[k1] + rows


def _fourier_b(tp, x, wf, bias, g, n1, n2, kc):
    b, s, c = x.shape
    _, g1, _, _ = _dft_tables(n1, n2)
    g1 = jnp.asarray(g1).astype(BF16)
    xv = x.reshape(b, n1, n2, c)
    n_lt = c // LANES
    pitch = n1 + SUBLANES
    kern = functools.partial(_fourier_b_kernel, n1=n1, kc=kc, c=c)
    out = pl.pallas_call(
        kern,
        grid=(b, n2 // kc),
        in_specs=_lane_specs(n1, kc, n_lt) + [
            pl.BlockSpec((None, n1, kc, c), lambda i, j: (i, 0, j, 0)),
            _const_spec(g1.shape),
            _const_spec(wf.shape),
            _const_spec(bias.shape),
            _const_spec(g.shape),
        ],
        out_specs=pl.BlockSpec((None, n1, kc, c), lambda i, j: (i, 0, j, 0)),
        out_shape=jax.ShapeDtypeStruct(xv.shape, F32),
        scratch_shapes=[pltpu.VMEM((kc * n1, 2 * c), BF16),
                        pltpu.VMEM((n_lt, kc * pitch, LANES), F32),
                        pltpu.VMEM((n_lt, n1 * kc, LANES), jnp.uint32)],
        compiler_params=_params(2),
        name="fourier_b",
    )(*([tp] * n_lt), xv, g1, wf, bias, g)
    return out.reshape(b, s, c)


def _fold_kernel(cc_ref, sc_ref, w_ref, out_ref):
    w = w_ref[...]
    out_ref[0] = jnp.dot(cc_ref[...], w, preferred_element_type=F32,
                         precision=lax.Precision.HIGHEST).astype(BF16)
    out_ref[1] = jnp.dot(sc_ref[...], w, preferred_element_type=F32,
                         precision=lax.Precision.HIGHEST).astype(BF16)


def _fold_channel_dft(w_out):
    d = w_out.shape[0]
    gd = d // N_GROUPS
    idx = np.arange(gd)
    ang = 2.0 * np.pi * np.outer(idx, idx) / gd
    cc = jnp.asarray(np.cos(ang).astype(np.float32))
    sc = jnp.asarray(np.sin(ang).astype(np.float32))
    out = pl.pallas_call(
        _fold_kernel,
        grid=(N_GROUPS,),
        in_specs=[
            pl.BlockSpec((gd, gd), lambda i: (0, 0)),
            pl.BlockSpec((gd, gd), lambda i: (0, 0)),
            pl.BlockSpec((gd, d), lambda i: (i, 0)),
        ],
        out_specs=pl.BlockSpec((2, gd, d), lambda i: (0, i, 0)),
        out_shape=jax.ShapeDtypeStruct((2, d, d), BF16),
        compiler_params=_params(1),
        name="fold_channel_dft",
    )(cc, sc, w_out)
    return out.reshape(2 * d, d)


def _ffn_kernel(*refs, with_attn, d_ff):
    if with_attn:
        x_ref, o_ref, wo_ref, g_ref, wg_ref, wu_ref, wd_ref, out_ref, a_scr = refs
    else:
        x_ref, g_ref, wg_ref, wu_ref, wd_ref, out_ref, a_scr = refs
    x = x_ref[...]
    if with_attn:
        m = jnp.dot(o_ref[...], wo_ref[...], preferred_element_type=F32)
        x = x + _rms(m, g_ref[1:2, :])
    hn = _rms(x, g_ref[2:3, :]).astype(BF16)
    for f in range(0, d_ff, FF_CHUNK):
        gate = jnp.dot(hn, wg_ref[:, f:f + FF_CHUNK], preferred_element_type=F32)
        up = jnp.dot(hn, wu_ref[:, f:f + FF_CHUNK], preferred_element_type=F32)
        a_scr[:, f:f + FF_CHUNK] = (gate * jax.nn.sigmoid(gate) * up).astype(BF16)
    y = jnp.dot(a_scr[...], wd_ref[...], preferred_element_type=F32)
    out_ref[...] = x + _rms(y, g_ref[3:4, :])


def _ffn(x, g, wg, wu, wd, attn=None, wo=None):
    b, s, c = x.shape
    d_ff = wg.shape[1]
    assert d_ff % FF_CHUNK == 0 and s % ROW_TILE == 0
    tm = ROW_TILE
    row = lambda i, j: (i, j, 0)
    with_attn = attn is not None
    in_specs = [pl.BlockSpec((None, tm, c), row)]
    args = [x]
    if with_attn:
        in_specs += [pl.BlockSpec((None, tm, c), row), _const_spec(wo.shape)]
        args += [attn, wo]
    in_specs += [_const_spec(g.shape), _const_spec(wg.shape), _const_spec(wu.shape),
                 _const_spec(wd.shape)]
    args += [g, wg, wu, wd]
    kern = functools.partial(_ffn_kernel, with_attn=with_attn, d_ff=d_ff)
    return pl.pallas_call(
        kern,
        grid=(b, s // tm),
        in_specs=in_specs,
        out_specs=pl.BlockSpec((None, tm, c), row),
        out_shape=jax.ShapeDtypeStruct(x.shape, F32),
        scratch_shapes=[pltpu.VMEM((tm, d_ff), BF16)],
        compiler_params=_params(2),
        name="ffn_attn" if with_attn else "ffn",
    )(*args)


def _mla_proj_kernel(x_ref, g_ref, wdn_ref, qn_ref, kvn_ref, wuqt_ref, wuk_ref, wuvt_ref,
                     cosf_ref, sinf_ref, cost_ref, sint_ref, k_out, q_out, v_out):
    tm = x_ref.shape[0]
    xn = _rms(x_ref[...], g_ref[0:1, :]).astype(BF16)
    down = jnp.dot(xn, wdn_ref[...], preferred_element_type=F32)
    c_q = down[:, :Q_LORA]
    c_kv = down[:, Q_LORA:Q_LORA + KV_LORA]
    kr = down[:, Q_LORA + KV_LORA:Q_LORA + KV_LORA + ROPE]
    kr_swapped = down[:, Q_LORA + KV_LORA + ROPE:]
    k_rope = kr * cosf_ref[...] + kr_swapped * sinf_ref[...]
    k_tail = jnp.concatenate([k_rope, jnp.zeros_like(k_rope)], axis=1).astype(BF16)
    cqn = _rms(c_q, qn_ref[...]).astype(BF16)
    ckvn = _rms(c_kv, kvn_ref[...]).astype(BF16)

    k_nope = jnp.dot(ckvn, wuk_ref[...], preferred_element_type=F32)
    for h in range(N_HEADS):
        k_out[h, :, :NOPE] = k_nope[:, h * NOPE:(h + 1) * NOPE].astype(BF16)
        k_out[h, :, NOPE:] = k_tail

    qt = lax.dot_general(wuqt_ref[...], cqn, NT_DIMS, preferred_element_type=F32) * Q_SCALE
    cos_t = cost_ref[...]
    sin_t = sint_ref[...]
    half = ROPE // 2
    hd = NOPE + ROPE
    for h in range(N_HEADS):
        base = h * hd
        q_out[h, :NOPE, :] = qt[base:base + NOPE].astype(BF16)
        x1 = qt[base + NOPE:base + NOPE + half]
        x2 = qt[base + NOPE + half:base + hd]
        q_out[h, NOPE:NOPE + half, :] = (x1 * cos_t - x2 * sin_t).astype(BF16)
        q_out[h, NOPE + half:hd, :] = (x2 * cos_t + x1 * sin_t).astype(BF16)
        q_out[h, hd:, :] = jnp.zeros((QK_PAD - hd, tm), BF16)

    vt = lax.dot_general(wuvt_ref[...], ckvn, NT_DIMS, preferred_element_type=F32)
    ones_rows = (lax.broadcasted_iota(jnp.int32, (V_ROWS - VDIM, tm), 0) == 0).astype(BF16)
    for h in range(N_HEADS):
        v_out[h, 0, :VDIM, :] = vt[h * VDIM:(h + 1) * VDIM].astype(BF16)
        v_out[h, 0, VDIM:, :] = ones_rows


def _mla_proj(x, g, wdn, qn, kvn, wuqt, wuk, wuvt, cosf, sinf, cos_t, sin_t):
    b, s, c = x.shape
    tm = KV_TILE
    assert s % tm == 0
    return pl.pallas_call(
        _mla_proj_kernel,
        grid=(b, s // tm),
        in_specs=[
            pl.BlockSpec((None, tm, c), lambda i, j: (i, j, 0)),
            _const_spec(g.shape), _const_spec(wdn.shape), _const_spec(qn.shape),
            _const_spec(kvn.shape), _const_spec(wuqt.shape), _const_spec(wuk.shape),
            _const_spec(wuvt.shape),
            pl.BlockSpec((tm, ROPE), lambda i, j: (j, 0)),
            pl.BlockSpec((tm, ROPE), lambda i, j: (j, 0)),
            pl.BlockSpec((ROPE // 2, tm), lambda i, j: (0, j)),
            pl.BlockSpec((ROPE // 2, tm), lambda i, j: (0, j)),
        ],
        out_specs=[
            pl.BlockSpec((None, N_HEADS, tm, QK_PAD), lambda i, j: (i, 0, j, 0)),
            pl.BlockSpec((None, N_HEADS, QK_PAD, tm), lambda i, j: (i, 0, 0, j)),
            pl.BlockSpec((None, N_HEADS, 1, V_ROWS, tm), lambda i, j: (i, 0, j, 0, 0)),
        ],
        out_shape=[
            jax.ShapeDtypeStruct((b, N_HEADS, s, QK_PAD), BF16),
            jax.ShapeDtypeStruct((b, N_HEADS, QK_PAD, s), BF16),
            jax.ShapeDtypeStruct((b, N_HEADS, s // tm, V_ROWS, tm), BF16),
        ],
        compiler_params=_params(2),
        name="mla_proj",
    )(x, g, wdn, qn, kvn, wuqt, wuk, wuvt, cosf, sinf, cos_t, sin_t)


def _attn_kernel(q_ref, qnext_ref, k_ref, v_ref, o_ref, acc_ref, s_ref, cmax_ref, *, nk, tk, nb):
    qt = q_ref[...]
    tq = qt.shape[1]
    acc_ref[...] = jnp.zeros_like(acc_ref)

    def scores(qmat, i, slot):
        start = i * tk if isinstance(i, int) else pl.multiple_of(i * tk, tk)
        k = k_ref[pl.ds(start, tk), :]
        s = jnp.dot(k, qmat, preferred_element_type=F32)
        s_ref[slot] = s
        cmax_ref[slot] = jnp.max(s.reshape(tk // SUBLANES, SUBLANES, s.shape[1]), axis=0)

    def consume(i, slot, m):
        s = s_ref[slot]
        m_new = jnp.maximum(m, jnp.max(cmax_ref[slot], axis=0, keepdims=True))
        alpha = jnp.exp2(m - m_new)
        p = jnp.exp2(s - m_new).astype(BF16)
        pv = jnp.dot(v_ref[i], p, preferred_element_type=F32)
        acc_ref[...] = alpha * acc_ref[...] + pv
        return m_new

    def super_block(base, m, last):
        for t in range(2 * nb):
            if last and t >= nb:
                scores(qnext_ref[...], t - nb, t - nb)
            else:
                scores(qt, base + t + nb, (t + nb) % (2 * nb))
            m = consume(base + t, t, m)
        return m

    @pl.when(pl.program_id(2) == 0)
    def _():
        for u in range(nb):
            scores(qt, u, u)

    n_super = nk // (2 * nb)
    m = jnp.full((1, tq), -jnp.inf, F32)
    m = lax.fori_loop(
        0, n_super - 1, lambda j, c: super_block(j * (2 * nb), c, False), m)
    super_block((n_super - 1) * (2 * nb), m, True)
    out =acc_ref[:VDIM, :] / acc_ref[VDIM:VDIM + 1, :]
    o_ref[...] = out.T.astype(BF16)


def _attention(qt, kc, vt):
    b, h, s, _ = kc.shape
    tq, tk = Q_TILE, KV_TILE
    nk = s // tk
    nb = SCORE_LOOKAHEAD
    assert nk % (2 * nb) == 0
    kern = functools.partial(_attn_kernel, nk=nk, tk=tk, nb=nb)
    nq = s // tq
    return pl.pallas_call(
        kern,
        grid=(b, h, nq),
        in_specs=[
            pl.BlockSpec((None, None, QK_PAD, tq), lambda i, j, q: (i, j, 0, q)),
            pl.BlockSpec((None, None, QK_PAD, tq),
                         lambda i, j, q: (i, j, 0, jnp.minimum(q + 1, nq - 1))),
            pl.BlockSpec((None, None, s, QK_PAD), lambda i, j, q: (i, j, 0, 0)),
            pl.BlockSpec((None, None, nk, V_ROWS, tk), lambda i, j, q: (i, j, 0, 0, 0)),
        ],
        out_specs=pl.BlockSpec((None, tq, VDIM), lambda i, j, q: (i, q, j)),
        out_shape=jax.ShapeDtypeStruct((b, s, h * VDIM), BF16),
        scratch_shapes=[pltpu.VMEM((V_ROWS, tq), F32), pltpu.VMEM((2 * nb, tk, tq), F32),
                        pltpu.VMEM((2 * nb, SUBLANES, tq), F32)],
        compiler_params=pltpu.CompilerParams(
            dimension_semantics=("parallel", "parallel", "arbitrary"),
            vmem_limit_bytes=V7X_VMEM_LIMIT_BYTES),
        name="attention",
    )(qt, qt, kc, vt)


def _rope_tables(s):
    inv_freq = 1.0 / (ROPE_THETA ** (jnp.arange(0, ROPE, 2, dtype=F32) / ROPE))
    ang = jnp.arange(s, dtype=F32)[:, None] * inv_freq[None, :]
    cos, sin = jnp.cos(ang), jnp.sin(ang)
    cosf = jnp.concatenate([cos, cos], axis=1)
    sinf = jnp.concatenate([-sin, sin], axis=1)
    return cosf, sinf, cos.T, sin.T


def _prep_weights(fnet_w_out, mla_w_down, mla_w_uq, mla_w_ukv, mla_w_o,
                  ffn_w_gate, ffn_w_up, ffn_w_down):
    w = {}
    w["wf"] = _fold_channel_dft(fnet_w_out[0])
    wd = mla_w_down[0]
    r0 = Q_LORA + KV_LORA
    half = ROPE // 2
    w["wdn"] = jnp.concatenate([wd, wd[:, r0 + half:r0 + ROPE], wd[:, r0:r0 + half]],
                               axis=1).astype(BF16)
    w["wuqt"] = mla_w_uq[0].T.astype(BF16)
    wukv = mla_w_ukv[0].reshape(KV_LORA, N_HEADS, NOPE + VDIM)
    w["wuk"] = wukv[:, :, :NOPE].reshape(KV_LORA, N_HEADS * NOPE).astype(BF16)
    w["wuvt"] = wukv[:, :, NOPE:].reshape(KV_LORA, N_HEADS * VDIM).T.astype(BF16)
    w["wo"] = mla_w_o[0].astype(BF16)
    w["wg"] = ffn_w_gate.astype(BF16)
    w["wu"] = ffn_w_up.astype(BF16)
    w["wd"] = ffn_w_down.astype(BF16)
    return w


def _trunk(x, norm_g, fnet_b_out, mla_q_norm, mla_kv_norm, w):
    b, s, c = x.shape
    n1, n2 = _split_seq(s)
    kc = SUBLANES
    g0, g1 = norm_g[0], norm_g[1]
    tp = _fourier_a(x, g0, n1, n2, kc)
    x = _fourier_b(tp, x, w["wf"], fnet_b_out[0][None, :], g0, n1, n2, kc)
    x = _ffn(x, g0, w["wg"][0], w["wu"][0], w["wd"][0])
    cosf, sinf, cos_t, sin_t = _rope_tables(s)
    kc_, qt, vt = _mla_proj(x, g1, w["wdn"], mla_q_norm[0][None, :], mla_kv_norm[0][None, :],
                            w["wuqt"], w["wuk"], w["wuvt"], cosf, sinf, cos_t, sin_t)
    o = _attention(qt, kc_, vt)
    return _ffn(x, g1, w["wg"][1], w["wu"][1], w["wd"][1], attn=o, wo=w["wo"])


def kernel(x_prompt, x_sample, norm_g, fnet_w_out, fnet_b_out, mla_w_down, mla_q_norm, mla_w_uq,
           mla_kv_norm, mla_w_ukv, mla_w_o, ffn_w_gate, ffn_w_up, ffn_w_down):
    w = _prep_weights(fnet_w_out, mla_w_down, mla_w_uq, mla_w_ukv, mla_w_o,
                      ffn_w_gate, ffn_w_up, ffn_w_down)
    y_prompt = _trunk(x_prompt, norm_g, fnet_b_out, mla_q_norm, mla_kv_norm, w)
    y_sample = _trunk(x_sample, norm_g, fnet_b_out, mla_q_norm, mla_kv_norm, w)
    return (y_prompt, y_sample)
```

```python
import functools
import math

import numpy as np
import jax
import jax.numpy as jnp
from jax import lax
from jax.experimental import pallas as pl
from jax.experimental.pallas import tpu as pltpu

F32 = jnp.float32
BF16 = jnp.bfloat16

N_GROUPS = 4
N_HEADS = 8
NOPE = 128
ROPE = 64
VDIM = 128
V_ROWS = VDIM + 16
Q_LORA = 256
KV_LORA = 256
QK_PAD = 256
ROPE_THETA = 10000.0
NORM_EPS = 1e-6
Q_SCALE = math.log2(math.e) / math.sqrt(NOPE + ROPE)

V7X_VMEM_LIMIT_BYTES = 56 * 1024 * 1024
ROW_TILE = 512
Q_TILE = 1024
KV_TILE = 512
SCORE_LOOKAHEAD = 4
FF_CHUNK = 256
LANES = 128
SUBLANES = 8

NT_DIMS = (((1,), (1,)), ((), ()))
TN_DIMS = (((0,), (0,)), ((), ()))


def _rms(x, g):
    ms = jnp.mean(x * x, axis=-1, keepdims=True)
    return x * lax.rsqrt(ms + NORM_EPS) * g


def _const_spec(shape):
    return pl.BlockSpec(shape, lambda *_: (0,) * len(shape), pipeline_mode=pl.Buffered(1))


def _params(n_axes):
    return pltpu.CompilerParams(dimension_semantics=("parallel",) * n_axes,
                                vmem_limit_bytes=V7X_VMEM_LIMIT_BYTES)


def _split_seq(s):
    lg = int(round(math.log2(s)))
    assert 2 ** lg == s
    n1 = 2 ** ((lg + 1) // 2)
    return n1, s // n1


def _dft_tables(n1, n2):
    n = n1 * n2
    k2 = np.arange(n2)
    a2 = 2.0 * np.pi * np.outer(k2, k2) / n2
    f2 = np.concatenate([np.cos(a2), -np.sin(a2)], axis=0)
    k1 = np.arange(n1)
    a1 = 2.0 * np.pi * np.outer(k1, k1) / n1
    c1, s1 = np.cos(a1), np.sin(a1)
    g1 = np.block([[c1, s1], [-s1, c1]])
    at = 2.0 * np.pi * np.outer(k2, np.arange(n1)) / n
    return (f2.astype(np.float32), g1.astype(np.float32),
            np.cos(at).astype(np.float32), np.sin(at).astype(np.float32))


def _flatten_slabs(lane_refs, flat_ref):
    for t, r in enumerate(lane_refs):
        flat_ref[t] = r[...].reshape(r.shape[0] * r.shape[1], LANES)


def _gather_rows(flat_ref, start, count, stride):
    return jnp.concatenate(
        [flat_ref[t, pl.ds(start, count, stride=stride), :] for t in range(flat_ref.shape[0])],
        axis=1)


def _lane_specs(rows, kc, n_lane_tiles):
    return [pl.BlockSpec((None, rows, kc, LANES),
                         functools.partial(lambda i, j, t: (i, 0, j, t), t=t))
            for t in range(n_lane_tiles)]


def _pack_pair(re, im):
    hi = lax.bitcast_convert_type(re.astype(BF16).astype(F32), jnp.uint32)
    lo = lax.bitcast_convert_type(im.astype(BF16).astype(F32), jnp.uint32)
    return hi | (lo >> 16)


def _unpack_pair(w):
    re = lax.bitcast_convert_type(w & jnp.uint32(0xFFFF0000), F32)
    im = lax.bitcast_convert_type(w << 16, F32)
    return re.astype(BF16), im.astype(BF16)


def _fourier_a_kernel(*refs, n2, kc, c):
    n_lt = c // LANES
    g_ref, f2_ref, twc_ref, tws_ref, out_ref, flat_ref = refs[n_lt:]
    _flatten_slabs(refs[:n_lt], flat_ref)
    f2 = f2_ref[...]
    g = g_ref[0:1, :]
    for jj in range(kc):
        xs = _gather_rows(flat_ref, jj, n2, kc)
        xn = _rms(xs, g).astype(BF16)
        t = jnp.dot(f2, xn, preferred_element_type=F32)
        tr, ti = t[:n2], t[n2:]
        cw = twc_ref[:, jj:jj + 1]
        sw = tws_ref[:, jj:jj + 1]
        out_ref[jj] = _pack_pair(tr * cw + ti * sw, ti * cw - tr * sw)


def _fourier_a(x, g, n1, n2, kc):
    b, s, c = x.shape
    f2, _, twc, tws = _dft_tables(n1, n2)
    f2 = jnp.asarray(f2).astype(BF16)
    twc = jnp.asarray(twc).reshape(n2, n1 // kc, kc).transpose(1, 0, 2)
    tws = jnp.asarray(tws).reshape(n2, n1 // kc, kc).transpose(1, 0, 2)
    xv = x.reshape(b, n2, n1, c)
    n_lt = c // LANES
    kern = functools.partial(_fourier_a_kernel, n2=n2, kc=kc, c=c)
    return pl.pallas_call(
        kern,
        grid=(b, n1 // kc),
        in_specs=_lane_specs(n2, kc, n_lt) + [
            _const_spec(g.shape),
            _const_spec(f2.shape),
            pl.BlockSpec((None, n2, kc), lambda i, j: (j, 0, 0)),
            pl.BlockSpec((None, n2, kc), lambda i, j: (j, 0, 0)),
        ],
        out_specs=pl.BlockSpec((None, kc, n2, c), lambda i, j: (i, j, 0, 0)),
        out_shape=jax.ShapeDtypeStruct((b, n1, n2, c), jnp.uint32),
        scratch_shapes=[pltpu.VMEM((n_lt, n2 * kc, LANES), F32)],
        compiler_params=_params(2),
        name="fourier_a",
    )(*([xv] * n_lt), g, f2, twc, tws)


def _fourier_b_kernel(*refs, n1, kc, c):
    n_lt = c // LANES
    x_ref, g1_ref, wf_ref, b_ref, g_ref, out_ref, u_scr, y_scr, flat_ref = refs[n_lt:]
    _flatten_slabs(refs[:n_lt], flat_ref)
    g1 = g1_ref[...]
    for t in range(kc):
        tr, ti = _unpack_pair(_gather_rows(flat_ref, t, n1, kc))
        u = jnp.dot(g1, jnp.concatenate([tr, ti], axis=0),
                    preferred_element_type=F32)
        u_scr[t * n1:(t + 1) * n1, :c] = u[:n1].astype(BF16)
        u_scr[t * n1:(t + 1) * n1, c:] = u[n1:].astype(BF16)
    m = jnp.dot(u_scr[...], wf_ref[...], preferred_element_type=F32) + b_ref[...]
    y = _rms(m, g_ref[1:2, :])
    pitch = y_scr.shape[1] // kc
    for lt in range(n_lt):
        for t in range(kc):
            y_scr[lt, t * pitch:t * pitch + n1, :] = y[t * n1:(t + 1) * n1,
                                                       lt * LANES:(lt + 1) * LANES]
    for k1 in range(n1):
        rows = jnp.concatenate(
            [y_scr[lt, pl.ds(k1, kc, stride=pitch), :] for lt in range(n_lt)], axis=1)
        out_ref[k1] = x_ref[k1] + rows


def _fourier_b(tp, x, wf, bias, g, n1, n2, kc):
    b, s, c = x.shape
    _, g1, _, _ = _dft_tables(n1, n2)
    g1 = jnp.asarray(g1).astype(BF16)
    xv = x.reshape(b, n1, n2, c)
    n_lt = c // LANES
    pitch = n1 + SUBLANES
    kern = functools.partial(_fourier_b_kernel, n1=n1, kc=kc, c=c)
    out = pl.pallas_call(
        kern,
        grid=(b, n2 // kc),
        in_specs=_lane_specs(n1, kc, n_lt) + [
            pl.BlockSpec((None, n1, kc, c), lambda i, j: (i, 0, j, 0)),
            _const_spec(g1.shape),
            _const_spec(wf.shape),
            _const_spec(bias.shape),
            _const_spec(g.shape),
        ],
        out_specs=pl.BlockSpec((None, n1, kc, c), lambda i, j: (i, 0, j, 0)),
        out_shape=jax.ShapeDtypeStruct(xv.shape, F32),
        scratch_shapes=[pltpu.VMEM((kc * n1, 2 * c), BF16),
                        pltpu.VMEM((n_lt, kc * pitch, LANES), F32),
                        pltpu.VMEM((n_lt, n1 * kc, LANES), jnp.uint32)],
        compiler_params=_params(2),
        name="fourier_b",
    )(*([tp] * n_lt), xv, g1, wf, bias, g)
    return out.reshape(b, s, c)


def _fold_kernel(cc_ref, sc_ref, w_ref, out_ref):
    w = w_ref[...]
    out_ref[0] = jnp.dot(cc_ref[...], w, preferred_element_type=F32,
                         precision=lax.Precision.HIGHEST).astype(BF16)
    out_ref[1] = jnp.dot(sc_ref[...], w, preferred_element_type=F32,
                         precision=lax.Precision.HIGHEST).astype(BF16)


def _fold_channel_dft(w_out):
    d = w_out.shape[0]
    gd = d // N_GROUPS
    idx = np.arange(gd)
    ang = 2.0 * np.pi * np.outer(idx, idx) / gd
    cc = jnp.asarray(np.cos(ang).astype(np.float32))
    sc = jnp.asarray(np.sin(ang).astype(np.float32))
    out = pl.pallas_call(
        _fold_kernel,
        grid=(N_GROUPS,),
        in_specs=[
            pl.BlockSpec((gd, gd), lambda i: (0, 0)),
            pl.BlockSpec((gd, gd), lambda i: (0, 0)),
            pl.BlockSpec((gd, d), lambda i: (i, 0)),
        ],
        out_specs=pl.BlockSpec((2, gd, d), lambda i: (0, i, 0)),
        out_shape=jax.ShapeDtypeStruct((2, d, d), BF16),
        compiler_params=_params(1),
        name="fold_channel_dft",
    )(cc, sc, w_out)
    return out.reshape(2 * d, d)


def _ffn_kernel(*refs, with_attn, d_ff):
    if with_attn:
        x_ref, o_ref, wo_ref, g_ref, wg_ref, wu_ref, wd_ref, out_ref, a_scr = refs
    else:
        x_ref, g_ref, wg_ref, wu_ref, wd_ref, out_ref, a_scr = refs
    x = x_ref[...]
    if with_attn:
        m = lax.dot_general(o_ref[...], wo_ref[...], TN_DIMS, preferred_element_type=F32)
        x = x + _rms(m, g_ref[1:2, :])
    hn = _rms(x, g_ref[2:3, :]).astype(BF16)
    for f in range(0, d_ff, FF_CHUNK):
        gate = jnp.dot(hn, wg_ref[:, f:f + FF_CHUNK], preferred_element_type=F32)
        up = jnp.dot(hn, wu_ref[:, f:f + FF_CHUNK], preferred_element_type=F32)
        a_scr[:, f:f + FF_CHUNK] = (gate * jax.nn.sigmoid(gate) * up).astype(BF16)
    y = jnp.dot(a_scr[...], wd_ref[...], preferred_element_type=F32)
    out_ref[...] = x + _rms(y, g_ref[3:4, :])


def _ffn(x, g, wg, wu, wd, attn=None, wo=None):
    b, s, c = x.shape
    d_ff = wg.shape[1]
    assert d_ff % FF_CHUNK == 0 and s % ROW_TILE == 0
    tm = ROW_TILE
    row = lambda i, j: (i, j, 0)
    with_attn = attn is not None
    in_specs = [pl.BlockSpec((None, tm, c), row)]
    args = [x]
    if with_attn:
        in_specs += [pl.BlockSpec((None, c, tm), lambda i, j: (i, 0, j)), _const_spec(wo.shape)]
        args += [attn, wo]
    in_specs += [_const_spec(g.shape), _const_spec(wg.shape), _const_spec(wu.shape),
                 _const_spec(wd.shape)]
    args += [g, wg, wu, wd]
    kern = functools.partial(_ffn_kernel, with_attn=with_attn, d_ff=d_ff)
    return pl.pallas_call(
        kern,
        grid=(b, s // tm),
        in_specs=in_specs,
        out_specs=pl.BlockSpec((None, tm, c), row),
        out_shape=jax.ShapeDtypeStruct(x.shape, F32),
        scratch_shapes=[pltpu.VMEM((tm, d_ff), BF16)],
        compiler_params=_params(2),
        name="ffn_attn" if with_attn else "ffn",
    )(*args)


def _mla_proj_kernel(x_ref, g_ref, wdn_ref, qn_ref, kvn_ref, wuqt_ref, wuk_ref, wuvt_ref,
                     cosf_ref, sinf_ref, cost_ref, sint_ref, k_out, q_out, v_out):
    tm = x_ref.shape[0]
    xn = _rms(x_ref[...], g_ref[0:1, :]).astype(BF16)
    down = jnp.dot(xn, wdn_ref[...], preferred_element_type=F32)
    c_q = down[:, :Q_LORA]
    c_kv = down[:, Q_LORA:Q_LORA + KV_LORA]
    kr = down[:, Q_LORA + KV_LORA:Q_LORA + KV_LORA + ROPE]
    kr_swapped = down[:, Q_LORA + KV_LORA + ROPE:]
    k_rope = kr * cosf_ref[...] + kr_swapped * sinf_ref[...]
    k_tail = jnp.concatenate([k_rope, jnp.zeros_like(k_rope)], axis=1).astype(BF16)
    cqn = _rms(c_q, qn_ref[...]).astype(BF16)
    ckvn = _rms(c_kv, kvn_ref[...]).astype(BF16)

    k_nope = jnp.dot(ckvn, wuk_ref[...], preferred_element_type=F32)
    for h in range(N_HEADS):
        k_out[h, :, :NOPE] = k_nope[:, h * NOPE:(h + 1) * NOPE].astype(BF16)
        k_out[h, :, NOPE:] = k_tail

    qt = lax.dot_general(wuqt_ref[...], cqn, NT_DIMS, preferred_element_type=F32) * Q_SCALE
    cos_t = cost_ref[...]
    sin_t = sint_ref[...]
    half = ROPE // 2
    hd = NOPE + ROPE
    for h in range(N_HEADS):
        base = h * hd
        q_out[h, :NOPE, :] = qt[base:base + NOPE].astype(BF16)
        x1 = qt[base + NOPE:base + NOPE + half]
        x2 = qt[base + NOPE + half:base + hd]
        q_out[h, NOPE:NOPE + half, :] = (x1 * cos_t - x2 * sin_t).astype(BF16)
        q_out[h, NOPE + half:hd, :] = (x2 * cos_t + x1 * sin_t).astype(BF16)
        q_out[h, hd:, :] = jnp.zeros((QK_PAD - hd, tm), BF16)

    vt = lax.dot_general(wuvt_ref[...], ckvn, NT_DIMS, preferred_element_type=F32)
    ones_rows = (lax.broadcasted_iota(jnp.int32, (V_ROWS - VDIM, tm), 0) == 0).astype(BF16)
    for h in range(N_HEADS):
        v_out[h, 0, :VDIM, :] = vt[h * VDIM:(h + 1) * VDIM].astype(BF16)
        v_out[h, 0, VDIM:, :] = ones_rows


def _mla_proj(x, g, wdn, qn, kvn, wuqt, wuk, wuvt, cosf, sinf, cos_t, sin_t):
    b, s, c = x.shape
    tm = KV_TILE
    assert s % tm == 0
    return pl.pallas_call(
        _mla_proj_kernel,
        grid=(b, s // tm),
        in_specs=[
            pl.BlockSpec((None, tm, c), lambda i, j: (i, j, 0)),
            _const_spec(g.shape), _const_spec(wdn.shape), _const_spec(qn.shape),
            _const_spec(kvn.shape), _const_spec(wuqt.shape), _const_spec(wuk.shape),
            _const_spec(wuvt.shape),
            pl.BlockSpec((tm, ROPE), lambda i, j: (j, 0)),
            pl.BlockSpec((tm, ROPE), lambda i, j: (j, 0)),
            pl.BlockSpec((ROPE // 2, tm), lambda i, j: (0, j)),
            pl.BlockSpec((ROPE // 2, tm), lambda i, j: (0, j)),
        ],
        out_specs=[
            pl.BlockSpec((None, N_HEADS, tm, QK_PAD), lambda i, j: (i, 0, j, 0)),
            pl.BlockSpec((None, N_HEADS, QK_PAD, tm), lambda i, j: (i, 0, 0, j)),
            pl.BlockSpec((None, N_HEADS, 1, V_ROWS, tm), lambda i, j: (i, 0, j, 0, 0)),
        ],
        out_shape=[
            jax.ShapeDtypeStruct((b, N_HEADS, s, QK_PAD), BF16),
            jax.ShapeDtypeStruct((b, N_HEADS, QK_PAD, s), BF16),
            jax.ShapeDtypeStruct((b, N_HEADS, s // tm, V_ROWS, tm), BF16),
        ],
        compiler_params=_params(2),
        name="mla_proj",
    )(x, g, wdn, qn, kvn, wuqt, wuk, wuvt, cosf, sinf, cos_t, sin_t)


def _attn_kernel(q_ref, qnext_ref, k_ref, v_ref, o_ref, acc_ref, s_ref, cmax_ref, *, nk, tk, nb):
    qt = q_ref[...]
    tq = qt.shape[1]
    acc_ref[...] = jnp.zeros_like(acc_ref)

    def scores(qmat, i, slot):
        start = i * tk if isinstance(i, int) else pl.multiple_of(i * tk, tk)
        k = k_ref[pl.ds(start, tk), :]
        s = jnp.dot(k, qmat, preferred_element_type=F32)
        s_ref[slot] = s
        cmax_ref[slot] = jnp.max(s.reshape(tk // SUBLANES, SUBLANES, s.shape[1]), axis=0)

    def consume(i, slot, m):
        s = s_ref[slot]
        m_new = jnp.maximum(m, jnp.max(cmax_ref[slot], axis=0, keepdims=True))
        alpha = jnp.exp2(m - m_new)
        p = jnp.exp2(s - m_new).astype(BF16)
        pv = jnp.dot(v_ref[i], p, preferred_element_type=F32)
        acc_ref[...] = alpha * acc_ref[...] + pv
        return m_new

    def super_block(base, m, last):
        for t in range(2 * nb):
            if last and t >= nb:
                scores(qnext_ref[...], t - nb, t - nb)
            else:
                scores(qt, base + t + nb, (t + nb) % (2 * nb))
            m = consume(base + t, t, m)
        return m

    @pl.when(pl.program_id(2) == 0)
    def _():
        for u in range(nb):
            scores(qt, u, u)

    n_super = nk // (2 * nb)
    m = jnp.full((1, tq), -jnp.inf, F32)
    m = lax.fori_loop(
        0, n_super - 1, lambda j, c: super_block(j * (2 * nb), c, False), m)
    super_block((n_super - 1) * (2 * nb), m, True)
    out = acc_ref[:VDIM, :] / acc_ref[VDIM:VDIM + 1, :]
    o_ref[...] = out.astype(BF16)


def _attention(qt, kc, vt):
    b, h, s, _ = kc.shape
    tq, tk = Q_TILE, KV_TILE
    nk = s // tk
    nb = SCORE_LOOKAHEAD
    assert nk % (2 * nb) == 0
    kern = functools.partial(_attn_kernel, nk=nk, tk=tk, nb=nb)
    nq = s // tq
    return pl.pallas_call(
        kern,
        grid=(b, h, nq),
        in_specs=[
            pl.BlockSpec((None, None, QK_PAD, tq), lambda i, j, q: (i, j, 0, q)),
            pl.BlockSpec((None, None, QK_PAD, tq),
                         lambda i, j, q: (i, j, 0, jnp.minimum(q + 1, nq - 1))),
            pl.BlockSpec((None, None, s, QK_PAD), lambda i, j, q: (i, j, 0, 0)),
            pl.BlockSpec((None, None, nk, V_ROWS, tk), lambda i, j, q: (i, j, 0, 0, 0)),
        ],
        out_specs=pl.BlockSpec((None, VDIM, tq), lambda i, j, q: (i, j, q)),
        out_shape=jax.ShapeDtypeStruct((b, h * VDIM, s), BF16),
        scratch_shapes=[pltpu.VMEM((V_ROWS, tq), F32), pltpu.VMEM((2 * nb, tk, tq), F32),
                        pltpu.VMEM((2 * nb, SUBLANES, tq), F32)],
        compiler_params=pltpu.CompilerParams(
            dimension_semantics=("parallel", "parallel", "arbitrary"),
            vmem_limit_bytes=V7X_VMEM_LIMIT_BYTES),
        name="attention",
    )(qt, qt, kc, vt)


def _rope_tables(s):
    inv_freq = 1.0 / (ROPE_THETA ** (np.arange(0, ROPE, 2, dtype=np.float64) / ROPE))
    ang = np.arange(s, dtype=np.float64)[:, None] * inv_freq[None, :]
    cos, sin = np.cos(ang).astype(np.float32), np.sin(ang).astype(np.float32)
    cosf = np.concatenate([cos, cos], axis=1)
    sinf = np.concatenate([-sin, sin], axis=1)
    return (jnp.asarray(cosf), jnp.asarray(sinf),
            jnp.asarray(np.ascontiguousarray(cos.T)), jnp.asarray(np.ascontiguousarray(sin.T)))


def _prep_weights(fnet_w_out, mla_w_down, mla_w_uq, mla_w_ukv, mla_w_o,
                  ffn_w_gate, ffn_w_up, ffn_w_down):
    w = {}
    w["wf"] = _fold_channel_dft(fnet_w_out[0])
    wd = mla_w_down[0]
    r0 = Q_LORA + KV_LORA
    half = ROPE // 2
    w["wdn"] = jnp.concatenate([wd, wd[:, r0 + half:r0 + ROPE], wd[:, r0:r0 + half]],
                               axis=1).astype(BF16)
    w["wuqt"] = mla_w_uq[0].T.astype(BF16)
    wukv = mla_w_ukv[0].reshape(KV_LORA, N_HEADS, NOPE + VDIM)
    w["wuk"] = wukv[:, :, :NOPE].reshape(KV_LORA, N_HEADS * NOPE).astype(BF16)
    w["wuvt"] = wukv[:, :, NOPE:].reshape(KV_LORA, N_HEADS * VDIM).T.astype(BF16)
    w["wo"] = mla_w_o[0].astype(BF16)
    w["wg"] = ffn_w_gate.astype(BF16)
    w["wu"] = ffn_w_up.astype(BF16)
    w["wd"] = ffn_w_down.astype(BF16)
    return w


def _trunk(x, norm_g, fnet_b_out, mla_q_norm, mla_kv_norm, w):
    b, s, c = x.shape
    n1, n2 = _split_seq(s)
    kc = SUBLANES
    g0, g1 = norm_g[0], norm_g[1]
    tp = _fourier_a(x, g0, n1, n2, kc)
    x = _fourier_b(tp, x, w["wf"], fnet_b_out[0][None, :], g0, n1, n2, kc)
    x = _ffn(x, g0, w["wg"][0], w["wu"][0], w["wd"][0])
    cosf, sinf, cos_t, sin_t = _rope_tables(s)
    kc_, qt, vt = _mla_proj(x, g1, w["wdn"], mla_q_norm[0][None, :], mla_kv_norm[0][None, :],
                            w["wuqt"], w["wuk"], w["wuvt"], cosf, sinf, cos_t, sin_t)
    o = _attention(qt, kc_, vt)
    return _ffn(x, g1, w["wg"][1], w["wu"][1], w["wd"][1], attn=o, wo=w["wo"])


def kernel(x_prompt, x_sample, norm_g, fnet_w_out, fnet_b_out, mla_w_down, mla_q_norm, mla_w_uq,
           mla_kv_norm, mla_w_ukv, mla_w_o, ffn_w_gate, ffn_w_up, ffn_w_down):
    w = _prep_weights(fnet_w_out, mla_w_down, mla_w_uq, mla_w_ukv, mla_w_o,
                      ffn_w_gate, ffn_w_up, ffn_w_down)
    y_prompt = _trunk(x_prompt, norm_g, fnet_b_out, mla_q_norm, mla_kv_norm, w)
    y_sample = _trunk(x_sample, norm_g, fnet_b_out, mla_q_norm, mla_kv_norm, w)
    return (y_prompt, y_sample)
```

```python
import functools
import math

import numpy as np
import jax
import jax.numpy as jnp
from jax import lax
from jax.experimental import pallas as pl
from jax.experimental.pallas import tpu as pltpu

F32 = jnp.float32
BF16 = jnp.bfloat16

N_GROUPS = 4
N_HEADS = 8
NOPE = 128
ROPE = 64
VDIM = 128
V_ROWS = VDIM + 16
Q_LORA = 256
KV_LORA = 256
QK_PAD = 256
ROPE_THETA = 10000.0
NORM_EPS = 1e-6
Q_SCALE = math.log2(math.e) / math.sqrt(NOPE + ROPE)

V7X_VMEM_LIMIT_BYTES = 56 * 1024 * 1024
ROW_TILE = 512
Q_TILE = 1024
KV_TILE = 512
SCORE_LOOKAHEAD = 4
FF_CHUNK = 256
N_PROJ_OPERANDS = 11
LANES = 128
SUBLANES = 8

NT_DIMS = (((1,), (1,)), ((), ()))
TN_DIMS = (((0,), (0,)), ((), ()))


def _rms(x, g):
    ms = jnp.mean(x * x, axis=-1, keepdims=True)
    return x * lax.rsqrt(ms + NORM_EPS) * g


def _const_spec(shape):
    return pl.BlockSpec(shape, lambda *_: (0,) * len(shape), pipeline_mode=pl.Buffered(1))


def _params(n_axes):
    return pltpu.CompilerParams(dimension_semantics=("parallel",) * n_axes,
                                vmem_limit_bytes=V7X_VMEM_LIMIT_BYTES)


def _split_seq(s):
    lg = int(round(math.log2(s)))
    assert 2 ** lg == s
    n1 = 2 ** ((lg + 1) // 2)
    return n1, s // n1


def _dft_tables(n1, n2):
    n = n1 * n2
    k2 = np.arange(n2)
    a2 = 2.0 * np.pi * np.outer(k2, k2) / n2
    f2 = np.concatenate([np.cos(a2), -np.sin(a2)], axis=0)
    k1 = np.arange(n1)
    a1 = 2.0 * np.pi * np.outer(k1, k1) / n1
    c1, s1 = np.cos(a1), np.sin(a1)
    g1 = np.block([[c1, s1], [-s1, c1]])
    at = 2.0 * np.pi * np.outer(k2, np.arange(n1)) / n
    return (f2.astype(np.float32), g1.astype(np.float32),
            np.cos(at).astype(np.float32), np.sin(at).astype(np.float32))


def _flatten_slabs(lane_refs, flat_ref):
    for t, r in enumerate(lane_refs):
        flat_ref[t] = r[...].reshape(r.shape[0] * r.shape[1], LANES)


def _gather_rows(flat_ref, start, count, stride):
    return jnp.concatenate(
        [flat_ref[t, pl.ds(start, count, stride=stride), :] for t in range(flat_ref.shape[0])],
        axis=1)


def _lane_specs(rows, kc, n_lane_tiles):
    return [pl.BlockSpec((None, rows, kc, LANES),
                         functools.partial(lambda i, j, t: (i, 0, j, t), t=t))
            for t in range(n_lane_tiles)]


def _pack_pair(re, im):
    hi = lax.bitcast_convert_type(re.astype(BF16).astype(F32), jnp.uint32)
    lo = lax.bitcast_convert_type(im.astype(BF16).astype(F32), jnp.uint32)
    return hi | (lo >> 16)


def _unpack_pair(w):
    re = lax.bitcast_convert_type(w & jnp.uint32(0xFFFF0000), F32)
    im = lax.bitcast_convert_type(w << 16, F32)
    return re.astype(BF16), im.astype(BF16)


def _fourier_a_kernel(*refs, n2, kc, c):
    n_lt = c // LANES
    g_ref, f2_ref, twc_ref, tws_ref, out_ref, flat_ref = refs[n_lt:]
    _flatten_slabs(refs[:n_lt], flat_ref)
    f2 = f2_ref[...]
    g = g_ref[0:1, :]
    for jj in range(kc):
        xs = _gather_rows(flat_ref, jj, n2, kc)
        xn = _rms(xs, g).astype(BF16)
        t = jnp.dot(f2, xn, preferred_element_type=F32)
        tr, ti = t[:n2], t[n2:]
        cw = twc_ref[:, jj:jj + 1]
        sw = tws_ref[:, jj:jj + 1]
        out_ref[jj] = _pack_pair(tr * cw + ti * sw, ti * cw - tr * sw)


def _fourier_a(x, g, n1, n2, kc):
    b, s, c = x.shape
    f2, _, twc, tws = _dft_tables(n1, n2)
    f2 = jnp.asarray(f2).astype(BF16)
    twc = jnp.asarray(twc).reshape(n2, n1 // kc, kc).transpose(1, 0, 2)
    tws = jnp.asarray(tws).reshape(n2, n1 // kc, kc).transpose(1, 0, 2)
    xv = x.reshape(b, n2, n1, c)
    n_lt = c // LANES
    kern = functools.partial(_fourier_a_kernel, n2=n2, kc=kc, c=c)
    return pl.pallas_call(
        kern,
        grid=(b, n1 // kc),
        in_specs=_lane_specs(n2, kc, n_lt) + [
            _const_spec(g.shape),
            _const_spec(f2.shape),
            pl.BlockSpec((None, n2, kc), lambda i, j: (j, 0, 0)),
            pl.BlockSpec((None, n2, kc), lambda i, j: (j, 0, 0)),
        ],
        out_specs=pl.BlockSpec((None, kc, n2, c), lambda i, j: (i, j, 0, 0)),
        out_shape=jax.ShapeDtypeStruct((b, n1, n2, c), jnp.uint32),
        scratch_shapes=[pltpu.VMEM((n_lt, n2 * kc, LANES), F32)],
        compiler_params=_params(2),
        name="fourier_a",
    )(*([xv] * n_lt), g, f2, twc, tws)


def _fourier_b_kernel(*refs, n1, kc, c):
    n_lt = c // LANES
    x_ref, g1_ref, wf_ref, b_ref, g_ref, out_ref, u_scr, y_scr, flat_ref = refs[n_lt:]
    _flatten_slabs(refs[:n_lt], flat_ref)
    g1 = g1_ref[...]
    for t in range(kc):
        tr, ti = _unpack_pair(_gather_rows(flat_ref, t, n1, kc))
        u = jnp.dot(g1, jnp.concatenate([tr, ti], axis=0),
                    preferred_element_type=F32)
        u_scr[t * n1:(t + 1) * n1, :c] = u[:n1].astype(BF16)
        u_scr[t * n1:(t + 1) * n1, c:] = u[n1:].astype(BF16)
    m = jnp.dot(u_scr[...], wf_ref[...], preferred_element_type=F32) + b_ref[...]
    y = _rms(m, g_ref[1:2, :])
    pitch = y_scr.shape[1] // kc
    for lt in range(n_lt):
        for t in range(kc):
            y_scr[lt, t * pitch:t * pitch + n1, :] = y[t * n1:(t + 1) * n1,
                                                       lt * LANES:(lt + 1) * LANES]
    for k1 in range(n1):
        rows = jnp.concatenate(
            [y_scr[lt, pl.ds(k1, kc, stride=pitch), :] for lt in range(n_lt)], axis=1)
        out_ref[k1] = x_ref[k1] + rows


def _fourier_b(tp, x, wf, bias, g, n1, n2, kc):
    b, s, c = x.shape
    _, g1, _, _ = _dft_tables(n1, n2)
    g1 = jnp.asarray(g1).astype(BF16)
    xv = x.reshape(b, n1, n2, c)
    n_lt = c // LANES
    pitch = n1 + SUBLANES
    kern = functools.partial(_fourier_b_kernel, n1=n1, kc=kc, c=c)
    out = pl.pallas_call(
        kern,
        grid=(b, n2 // kc),
        in_specs=_lane_specs(n1, kc, n_lt) + [
            pl.BlockSpec((None, n1, kc, c), lambda i, j: (i, 0, j, 0)),
            _const_spec(g1.shape),
            _const_spec(wf.shape),
            _const_spec(bias.shape),
            _const_spec(g.shape),
        ],
        out_specs=pl.BlockSpec((None, n1, kc, c), lambda i, j: (i, 0, j, 0)),
        out_shape=jax.ShapeDtypeStruct(xv.shape, F32),
        scratch_shapes=[pltpu.VMEM((kc * n1, 2 * c), BF16),
                        pltpu.VMEM((n_lt, kc * pitch, LANES), F32),
                        pltpu.VMEM((n_lt, n1 * kc, LANES), jnp.uint32)],
        compiler_params=_params(2),
        name="fourier_b",
    )(*([tp] * n_lt), xv, g1, wf, bias, g)
    return out.reshape(b, s, c)


def _fold_kernel(cc_ref, sc_ref, w_ref, out_ref):
    w = w_ref[...]
    out_ref[0] = jnp.dot(cc_ref[...], w, preferred_element_type=F32,
                         precision=lax.Precision.HIGHEST).astype(BF16)
    out_ref[1] = jnp.dot(sc_ref[...], w, preferred_element_type=F32,
                         precision=lax.Precision.HIGHEST).astype(BF16)


def _fold_channel_dft(w_out):
    d = w_out.shape[0]
    gd = d // N_GROUPS
    idx = np.arange(gd)
    ang = 2.0 * np.pi * np.outer(idx, idx) / gd
    cc = jnp.asarray(np.cos(ang).astype(np.float32))
    sc = jnp.asarray(np.sin(ang).astype(np.float32))
    out = pl.pallas_call(
        _fold_kernel,
        grid=(N_GROUPS,),
        in_specs=[
            pl.BlockSpec((gd, gd), lambda i: (0, 0)),
            pl.BlockSpec((gd, gd), lambda i: (0, 0)),
            pl.BlockSpec((gd, d), lambda i: (i, 0)),
        ],
        out_specs=pl.BlockSpec((2, gd, d), lambda i: (0, i, 0)),
        out_shape=jax.ShapeDtypeStruct((2, d, d), BF16),
        compiler_params=_params(1),
        name="fold_channel_dft",
    )(cc, sc, w_out)
    return out.reshape(2 * d, d)


def _ffn_kernel(*refs, with_attn, with_proj, d_ff):
    refs = list(refs)
    x_ref = refs.pop(0)
    if with_attn:
        o_ref, wo_ref = refs.pop(0), refs.pop(0)
    g_ref, wg_ref, wu_ref, wd_ref = refs[:4]
    proj_in = refs[4:4 + N_PROJ_OPERANDS] if with_proj else []
    refs = refs[4 + len(proj_in):]
    out_ref, a_scr = refs[0], refs[-1]
    proj_out = refs[1:-1]
    x = x_ref[...]
    if with_attn:
        m = lax.dot_general(o_ref[...], wo_ref[...], TN_DIMS, preferred_element_type=F32)
        x = x + _rms(m, g_ref[1:2, :])
    hn = _rms(x, g_ref[2:3, :]).astype(BF16)
    for f in range(0, d_ff, FF_CHUNK):
        gate = jnp.dot(hn, wg_ref[:, f:f + FF_CHUNK], preferred_element_type=F32)
        up = jnp.dot(hn, wu_ref[:, f:f + FF_CHUNK], preferred_element_type=F32)
        a_scr[:, f:f + FF_CHUNK] = (gate * jax.nn.sigmoid(gate) * up).astype(BF16)
    y = jnp.dot(a_scr[...], wd_ref[...], preferred_element_type=F32)
    x = x + _rms(y, g_ref[3:4, :])
    out_ref[...] = x
    if with_proj:
        _mla_project(x, *proj_in, *proj_out)


def _ffn(x, g, wg, wu, wd, attn=None, wo=None, proj_args=None):
    b, s, c = x.shape
    d_ff = wg.shape[1]
    assert d_ff % FF_CHUNK == 0 and s % ROW_TILE == 0
    tm = ROW_TILE
    row = lambda i, j: (i, j, 0)
    with_attn = attn is not None
    with_proj = proj_args is not None
    in_specs = [pl.BlockSpec((None, tm, c), row)]
    args = [x]
    if with_attn:
        in_specs += [pl.BlockSpec((None, c, tm), lambda i, j: (i, 0, j)), _const_spec(wo.shape)]
        args += [attn, wo]
    in_specs += [_const_spec(g.shape), _const_spec(wg.shape), _const_spec(wu.shape),
                 _const_spec(wd.shape)]
    args += [g, wg, wu, wd]
    out_specs = [pl.BlockSpec((None, tm, c), row)]
    out_shapes = [jax.ShapeDtypeStruct(x.shape, F32)]
    if with_proj:
        assert len(proj_args) == N_PROJ_OPERANDS and tm == KV_TILE
        proj_specs, proj_out_specs, proj_out_shapes = _mla_specs(b, s, tm, proj_args)
        in_specs += proj_specs
        args += list(proj_args)
        out_specs += proj_out_specs
        out_shapes += proj_out_shapes
    kern = functools.partial(_ffn_kernel, with_attn=with_attn, with_proj=with_proj, d_ff=d_ff)
    outs = pl.pallas_call(
        kern,
        grid=(b, s // tm),
        in_specs=in_specs,
        out_specs=out_specs,
        out_shape=out_shapes,
        scratch_shapes=[pltpu.VMEM((tm, d_ff), BF16)],
        compiler_params=_params(2),
        name="ffn_attn" if with_attn else ("ffn_proj" if with_proj else "ffn"),
    )(*args)
    return outs if with_proj else outs[0]


def _mla_project(x, g_ref, wdn_ref, qn_ref, kvn_ref, wuqt_ref, wuk_ref, wuvt_ref,
                 cosf_ref, sinf_ref, cost_ref, sint_ref, k_out, q_out, v_out):
    tm = x.shape[0]
    xn = _rms(x, g_ref[0:1, :]).astype(BF16)
    down = jnp.dot(xn, wdn_ref[...], preferred_element_type=F32)
    c_q = down[:, :Q_LORA]
    c_kv = down[:, Q_LORA:Q_LORA + KV_LORA]
    kr = down[:, Q_LORA + KV_LORA:Q_LORA + KV_LORA + ROPE]
    kr_swapped = down[:, Q_LORA + KV_LORA + ROPE:]
    k_rope = kr * cosf_ref[...] + kr_swapped * sinf_ref[...]
    k_tail = jnp.concatenate([k_rope, jnp.zeros_like(k_rope)], axis=1).astype(BF16)
    cqn = _rms(c_q, qn_ref[...]).astype(BF16)
    ckvn = _rms(c_kv, kvn_ref[...]).astype(BF16)

    k_nope = jnp.dot(ckvn, wuk_ref[...], preferred_element_type=F32)
    for h in range(N_HEADS):
        k_out[h, :, :NOPE] = k_nope[:, h * NOPE:(h + 1) * NOPE].astype(BF16)
        k_out[h, :, NOPE:] = k_tail

    qt = lax.dot_general(wuqt_ref[...], cqn, NT_DIMS, preferred_element_type=F32) * Q_SCALE
    cos_t = cost_ref[...]
    sin_t = sint_ref[...]
    half = ROPE // 2
    hd = NOPE + ROPE
    for h in range(N_HEADS):
        base = h * hd
        q_out[h, :NOPE, :] = qt[base:base + NOPE].astype(BF16)
        x1 = qt[base + NOPE:base + NOPE + half]
        x2 = qt[base + NOPE + half:base + hd]
        q_out[h, NOPE:NOPE + half, :] = (x1 * cos_t - x2 * sin_t).astype(BF16)
        q_out[h, NOPE + half:hd, :] = (x2 * cos_t + x1 * sin_t).astype(BF16)
        q_out[h, hd:, :] = jnp.zeros((QK_PAD - hd, tm), BF16)

    vt = lax.dot_general(wuvt_ref[...], ckvn, NT_DIMS, preferred_element_type=F32)
    ones_rows = (lax.broadcasted_iota(jnp.int32, (V_ROWS - VDIM, tm), 0) == 0).astype(BF16)
    for h in range(N_HEADS):
        v_out[h, 0, :VDIM, :] = vt[h * VDIM:(h + 1) * VDIM].astype(BF16)
        v_out[h, 0, VDIM:, :] = ones_rows


def _mla_specs(b, s, tm, proj_args):
    consts = [_const_spec(a.shape) for a in proj_args[:7]]
    tables = [pl.BlockSpec((tm, ROPE), lambda i, j: (j, 0)),
              pl.BlockSpec((tm, ROPE), lambda i, j: (j, 0)),
              pl.BlockSpec((ROPE // 2, tm), lambda i, j: (0, j)),
              pl.BlockSpec((ROPE // 2, tm), lambda i, j: (0, j))]
    out_specs = [
        pl.BlockSpec((None, N_HEADS, tm, QK_PAD), lambda i, j: (i, 0, j, 0)),
        pl.BlockSpec((None, N_HEADS, QK_PAD, tm), lambda i, j: (i, 0, 0, j)),
        pl.BlockSpec((None, N_HEADS, 1, V_ROWS, tm), lambda i, j: (i, 0, j, 0, 0)),
    ]
    out_shapes = [
        jax.ShapeDtypeStruct((b, N_HEADS, s, QK_PAD), BF16),
        jax.ShapeDtypeStruct((b, N_HEADS, QK_PAD, s), BF16),
        jax.ShapeDtypeStruct((b, N_HEADS, s // tm, V_ROWS, tm), BF16),
    ]
    return consts + tables, out_specs, out_shapes


def _attn_kernel(q_ref, qnext_ref, k_ref, v_ref, o_ref, acc_ref, s_ref, cmax_ref, *, nk, tk, nb):
    qt = q_ref[...]
    tq = qt.shape[1]
    acc_ref[...] = jnp.zeros_like(acc_ref)

    def scores(qmat, i, slot):
        start = i * tk if isinstance(i, int) else pl.multiple_of(i * tk, tk)
        k = k_ref[pl.ds(start, tk), :]
        s = jnp.dot(k, qmat, preferred_element_type=F32)
        s_ref[slot] = s
        cmax_ref[slot] = jnp.max(s.reshape(tk // SUBLANES, SUBLANES, s.shape[1]), axis=0)

    def consume(i, slot, m):
        s = s_ref[slot]
        m_new = jnp.maximum(m, jnp.max(cmax_ref[slot], axis=0, keepdims=True))
        alpha = jnp.exp2(m - m_new)
        p = jnp.exp2(s - m_new).astype(BF16)
        pv = jnp.dot(v_ref[i], p, preferred_element_type=F32)
        acc_ref[...] = alpha * acc_ref[...] + pv
        return m_new

    def super_block(base, m, last):
        for t in range(2 * nb):
            if last and t >= nb:
                scores(qnext_ref[...], t - nb, t - nb)
            else:
                scores(qt, base + t + nb, (t + nb) % (2 * nb))
            m = consume(base + t, t, m)
        return m

    @pl.when(pl.program_id(2) == 0)
    def _():
        for u in range(nb):
            scores(qt, u, u)

    n_super = nk // (2 * nb)
    m = jnp.full((1, tq), -jnp.inf, F32)
    m = lax.fori_loop(
        0, n_super - 1, lambda j, c: super_block(j * (2 * nb), c, False), m)
    super_block((n_super - 1) * (2 * nb), m, True)
    out = acc_ref[:VDIM, :] / acc_ref[VDIM:VDIM + 1, :]
    o_ref[...] = out.astype(BF16)


def _attention(qt, kc, vt):
    b, h, s, _ = kc.shape
    tq, tk = Q_TILE, KV_TILE
    nk = s // tk
    nb = SCORE_LOOKAHEAD
    assert nk % (2 * nb) == 0
    kern = functools.partial(_attn_kernel, nk=nk, tk=tk, nb=nb)
    nq = s // tq
    return pl.pallas_call(
        kern,
        grid=(b, h, nq),
        in_specs=[
            pl.BlockSpec((None, None, QK_PAD, tq), lambda i, j, q: (i, j, 0, q)),
            pl.BlockSpec((None, None, QK_PAD, tq),
                         lambda i, j, q: (i, j, 0, jnp.minimum(q + 1, nq - 1))),
            pl.BlockSpec((None, None, s, QK_PAD), lambda i, j, q: (i, j, 0, 0)),
            pl.BlockSpec((None, None, nk, V_ROWS, tk), lambda i, j, q: (i, j, 0, 0, 0)),
        ],
        out_specs=pl.BlockSpec((None, VDIM, tq), lambda i, j, q: (i, j, q)),
        out_shape=jax.ShapeDtypeStruct((b, h * VDIM, s), BF16),
        scratch_shapes=[pltpu.VMEM((V_ROWS, tq), F32), pltpu.VMEM((2 * nb, tk, tq), F32),
                        pltpu.VMEM((2 * nb, SUBLANES, tq), F32)],
        compiler_params=pltpu.CompilerParams(
            dimension_semantics=("parallel", "parallel", "arbitrary"),
            vmem_limit_bytes=V7X_VMEM_LIMIT_BYTES),
        name="attention",
    )(qt, qt, kc, vt)


def _rope_tables(s):
    inv_freq = 1.0 / (ROPE_THETA ** (np.arange(0, ROPE, 2, dtype=np.float64) / ROPE))
    ang = np.arange(s, dtype=np.float64)[:, None] * inv_freq[None, :]
    cos, sin = np.cos(ang).astype(np.float32), np.sin(ang).astype(np.float32)
    cosf = np.concatenate([cos, cos], axis=1)
    sinf = np.concatenate([-sin, sin], axis=1)
    return (jnp.asarray(cosf), jnp.asarray(sinf),
            jnp.asarray(np.ascontiguousarray(cos.T)), jnp.asarray(np.ascontiguousarray(sin.T)))


def _prep_weights(fnet_w_out, mla_w_down, mla_w_uq, mla_w_ukv, mla_w_o,
                  ffn_w_gate, ffn_w_up, ffn_w_down):
    w = {}
    w["wf"] = _fold_channel_dft(fnet_w_out[0])
    wd = mla_w_down[0]
    r0 = Q_LORA + KV_LORA
    half = ROPE // 2
    w["wdn"] = jnp.concatenate([wd, wd[:, r0 + half:r0 + ROPE], wd[:, r0:r0 + half]],
                               axis=1).astype(BF16)
    w["wuqt"] = mla_w_uq[0].T.astype(BF16)
    wukv = mla_w_ukv[0].reshape(KV_LORA, N_HEADS, NOPE + VDIM)
    w["wuk"] = wukv[:, :, :NOPE].reshape(KV_LORA, N_HEADS * NOPE).astype(BF16)
    w["wuvt"] = wukv[:, :, NOPE:].reshape(KV_LORA, N_HEADS * VDIM).T.astype(BF16)
    w["wo"] = mla_w_o[0].astype(BF16)
    w["wg"] = ffn_w_gate.astype(BF16)
    w["wu"] = ffn_w_up.astype(BF16)
    w["wd"] = ffn_w_down.astype(BF16)
    return w


def _trunk(x, norm_g, fnet_b_out, mla_q_norm, mla_kv_norm, w):
    b, s, c = x.shape
    n1, n2 = _split_seq(s)
    kc = SUBLANES
    g0, g1 = norm_g[0], norm_g[1]
    tp = _fourier_a(x, g0, n1, n2, kc)
    x = _fourier_b(tp, x, w["wf"], fnet_b_out[0][None, :], g0, n1, n2, kc)
    proj_args = (g1, w["wdn"], mla_q_norm[0][None, :], mla_kv_norm[0][None, :],
                 w["wuqt"], w["wuk"], w["wuvt"], *_rope_tables(s))
    x, kc_, qt, vt = _ffn(x, g0, w["wg"][0], w["wu"][0], w["wd"][0], proj_args=proj_args)
    o = _attention(qt, kc_, vt)
    return _ffn(x, g1, w["wg"][1], w["wu"][1], w["wd"][1], attn=o, wo=w["wo"])


def kernel(x_prompt, x_sample, norm_g, fnet_w_out, fnet_b_out, mla_w_down, mla_q_norm, mla_w_uq,
           mla_kv_norm, mla_w_ukv, mla_w_o, ffn_w_gate, ffn_w_up, ffn_w_down):
    w = _prep_weights(fnet_w_out, mla_w_down, mla_w_uq, mla_w_ukv, mla_w_o,
                      ffn_w_gate, ffn_w_up, ffn_w_down)
    y_prompt = _trunk(x_prompt, norm_g, fnet_b_out, mla_q_norm, mla_kv_norm, w)
    y_sample = _trunk(x_sample, norm_g, fnet_b_out, mla_q_norm, mla_kv_norm, w)
    return (y_prompt, y_sample)
```

```python
import functools
import math

import numpy as np
import jax
import jax.numpy as jnp
from jax import lax
from jax.experimental import pallas as pl
from jax.experimental.pallas import tpu as pltpu

F32 = jnp.float32
BF16 = jnp.bfloat16

N_GROUPS = 4
N_HEADS = 8
NOPE = 128
ROPE = 64
VDIM = 128
V_ROWS = VDIM + 16
Q_LORA = 256
KV_LORA = 256
QK_PAD = 256
ROPE_THETA = 10000.0
NORM_EPS = 1e-6
Q_SCALE = math.log2(math.e) / math.sqrt(NOPE + ROPE)

V7X_VMEM_LIMIT_BYTES = 56 * 1024 * 1024
ROW_TILE = 512
Q_TILE = 1024
KV_TILE = 512
SCORE_LOOKAHEAD = 4
FF_CHUNK = 256
N_PROJ_OPERANDS = 11
LANES = 128
SUBLANES = 8

NT_DIMS = (((1,), (1,)), ((), ()))
TN_DIMS = (((0,), (0,)), ((), ()))


def _rms(x, g):
    ms = jnp.mean(x * x, axis=-1, keepdims=True)
    return x * lax.rsqrt(ms + NORM_EPS) * g


def _const_spec(shape):
    return pl.BlockSpec(shape, lambda *_: (0,) * len(shape), pipeline_mode=pl.Buffered(1))


def _params(n_axes):
    return pltpu.CompilerParams(dimension_semantics=("parallel",) * n_axes,
                                vmem_limit_bytes=V7X_VMEM_LIMIT_BYTES)


def _split_seq(s):
    lg = int(round(math.log2(s)))
    assert 2 ** lg == s
    n1 = 2 ** ((lg + 1) // 2)
    return n1, s // n1


def _dft_tables(n1, n2):
    n = n1 * n2
    k2 = np.arange(n2)
    a2 = 2.0 * np.pi * np.outer(k2, k2) / n2
    f2 = np.concatenate([np.cos(a2), -np.sin(a2)], axis=0)
    k1 = np.arange(n1)
    a1 = 2.0 * np.pi * np.outer(k1, k1) / n1
    c1, s1 = np.cos(a1), np.sin(a1)
    g1 = np.block([[c1, s1], [-s1, c1]])
    at = 2.0 * np.pi * np.outer(k2, np.arange(n1)) / n
    return (f2.astype(np.float32), g1.astype(np.float32),
            np.cos(at).astype(np.float32), np.sin(at).astype(np.float32))


def _flatten_slabs(lane_refs, flat_ref):
    for t, r in enumerate(lane_refs):
        flat_ref[t] = r[...].reshape(r.shape[0] * r.shape[1], LANES)


def _gather_rows(flat_ref, start, count, stride):
    return jnp.concatenate(
        [flat_ref[t, pl.ds(start, count, stride=stride), :] for t in range(flat_ref.shape[0])],
        axis=1)


def _lane_specs(rows, kc, n_lane_tiles):
    return [pl.BlockSpec((None, rows, kc, LANES),
                         functools.partial(lambda i, j, t: (i, 0, j, t), t=t))
            for t in range(n_lane_tiles)]


def _pack_pair(re, im):
    hi = lax.bitcast_convert_type(re.astype(BF16).astype(F32), jnp.uint32)
    lo = lax.bitcast_convert_type(im.astype(BF16).astype(F32), jnp.uint32)
    return hi | (lo >> 16)


def _unpack_pair(w):
    re = lax.bitcast_convert_type(w & jnp.uint32(0xFFFF0000), F32)
    im = lax.bitcast_convert_type(w << 16, F32)
    return re.astype(BF16), im.astype(BF16)


def _fourier_a_kernel(*refs, n2, kc, c):
    n_lt = c // LANES
    g_ref, f2_ref, twc_ref, tws_ref, out_ref, flat_ref = refs[n_lt:]
    _flatten_slabs(refs[:n_lt], flat_ref)
    f2 = f2_ref[...]
    g = g_ref[0:1, :]
    for jj in range(kc):
        xs = _gather_rows(flat_ref, jj, n2, kc)
        xn = _rms(xs, g).astype(BF16)
        t = jnp.dot(f2, xn, preferred_element_type=F32)
        tr, ti = t[:n2], t[n2:]
        cw = twc_ref[:, jj:jj + 1]
        sw = tws_ref[:, jj:jj + 1]
        out_ref[jj] = _pack_pair(tr * cw + ti * sw, ti * cw - tr * sw)


def _fourier_a(x, g, n1, n2, kc):
    b, s, c = x.shape
    f2, _, twc, tws = _dft_tables(n1, n2)
    f2 = jnp.asarray(f2).astype(BF16)
    twc = jnp.asarray(twc).reshape(n2, n1 // kc, kc).transpose(1, 0, 2)
    tws = jnp.asarray(tws).reshape(n2, n1 // kc, kc).transpose(1, 0, 2)
    xv = x.reshape(b, n2, n1, c)
    n_lt = c // LANES
    kern = functools.partial(_fourier_a_kernel, n2=n2, kc=kc, c=c)
    return pl.pallas_call(
        kern,
        grid=(b, n1 // kc),
        in_specs=_lane_specs(n2, kc, n_lt) + [
            _const_spec(g.shape),
            _const_spec(f2.shape),
            pl.BlockSpec((None, n2, kc), lambda i, j: (j, 0, 0)),
            pl.BlockSpec((None, n2, kc), lambda i, j: (j, 0, 0)),
        ],
        out_specs=pl.BlockSpec((None, kc, n2, c), lambda i, j: (i, j, 0, 0)),
        out_shape=jax.ShapeDtypeStruct((b, n1, n2, c), jnp.uint32),
        scratch_shapes=[pltpu.VMEM((n_lt, n2 * kc, LANES), F32)],
        compiler_params=_params(2),
        name="fourier_a",
    )(*([xv] * n_lt), g, f2, twc, tws)


def _fourier_b_kernel(*refs, n1, kc, c):
    n_lt = c // LANES
    x_ref, g1_ref, wf_ref, b_ref, g_ref, out_ref, u_scr, y_scr, flat_ref = refs[n_lt:]
    _flatten_slabs(refs[:n_lt], flat_ref)
    g1 = g1_ref[...]
    hn1 = n1 // 2
    for t in range(kc):
        tr, ti = _unpack_pair(_gather_rows(flat_ref, t, n1, kc))
        u = jnp.dot(g1, jnp.concatenate([tr, ti], axis=0),
                    preferred_element_type=F32)
        for hh in range(2):
            dst = slice((hh * kc + t) * hn1, (hh * kc + t + 1) * hn1)
            u_scr[dst, :c] = u[hh * hn1:(hh + 1) * hn1].astype(BF16)
            u_scr[dst, c:] = u[n1 + hh * hn1:n1 + (hh + 1) * hn1].astype(BF16)
    pitch = y_scr.shape[1] // kc
    for hh in range(2):
        m = jnp.dot(u_scr[hh * kc * hn1:(hh + 1) * kc * hn1, :], wf_ref[...],
                    preferred_element_type=F32) + b_ref[...]
        y = _rms(m, g_ref[1:2, :])
        for lt in range(n_lt):
            for t in range(kc):
                y_scr[lt, t * pitch + hh * hn1:t * pitch + (hh + 1) * hn1, :] = (
                    y[t * hn1:(t + 1) * hn1, lt * LANES:(lt + 1) * LANES])
        for k1 in range(hh * hn1, (hh + 1) * hn1):
            rows = jnp.concatenate(
                [y_scr[lt, pl.ds(k1, kc, stride=pitch), :] for lt in range(n_lt)], axis=1)
            out_ref[k1] = x_ref[k1] + rows


def _fourier_b(tp, x, wf, bias, g, n1, n2, kc):
    b, s, c = x.shape
    _, g1, _, _ = _dft_tables(n1, n2)
    g1 = jnp.asarray(g1).astype(BF16)
    xv = x.reshape(b, n1, n2, c)
    n_lt = c // LANES
    pitch = n1 + SUBLANES
    kern = functools.partial(_fourier_b_kernel, n1=n1, kc=kc, c=c)
    out = pl.pallas_call(
        kern,
        grid=(b, n2 // kc),
        in_specs=_lane_specs(n1, kc, n_lt) + [
            pl.BlockSpec((None, n1, kc, c), lambda i, j: (i, 0, j, 0)),
            _const_spec(g1.shape),
            _const_spec(wf.shape),
            _const_spec(bias.shape),
            _const_spec(g.shape),
        ],
        out_specs=pl.BlockSpec((None, n1, kc, c), lambda i, j: (i, 0, j, 0)),
        out_shape=jax.ShapeDtypeStruct(xv.shape, F32),
        scratch_shapes=[pltpu.VMEM((kc * n1, 2 * c), BF16),
                        pltpu.VMEM((n_lt, kc * pitch, LANES), F32),
                        pltpu.VMEM((n_lt, n1 * kc, LANES), jnp.uint32)],
        compiler_params=_params(2),
        name="fourier_b",
    )(*([tp] * n_lt), xv, g1, wf, bias, g)
    return out.reshape(b, s, c)


def _fold_kernel(cc_ref, sc_ref, w_ref, out_ref):
    w = w_ref[...]
    out_ref[0] = jnp.dot(cc_ref[...], w, preferred_element_type=F32,
                         precision=lax.Precision.HIGHEST).astype(BF16)
    out_ref[1] = jnp.dot(sc_ref[...], w, preferred_element_type=F32,
                         precision=lax.Precision.HIGHEST).astype(BF16)


def _fold_channel_dft(w_out):
    d = w_out.shape[0]
    gd = d // N_GROUPS
    idx = np.arange(gd)
    ang = 2.0 * np.pi * np.outer(idx, idx) / gd
    cc = jnp.asarray(np.cos(ang).astype(np.float32))
    sc = jnp.asarray(np.sin(ang).astype(np.float32))
    out = pl.pallas_call(
        _fold_kernel,
        grid=(N_GROUPS,),
        in_specs=[
            pl.BlockSpec((gd, gd), lambda i: (0, 0)),
            pl.BlockSpec((gd, gd), lambda i: (0, 0)),
            pl.BlockSpec((gd, d), lambda i: (i, 0)),
        ],
        out_specs=pl.BlockSpec((2, gd, d), lambda i: (0, i, 0)),
        out_shape=jax.ShapeDtypeStruct((2, d, d), BF16),
        compiler_params=_params(1),
        name="fold_channel_dft",
    )(cc, sc, w_out)
    return out.reshape(2 * d, d)


def _ffn_kernel(*refs, with_attn, with_proj, d_ff):
    refs = list(refs)
    x_ref = refs.pop(0)
    if with_attn:
        o_ref, wo_ref = refs.pop(0), refs.pop(0)
    g_ref, wg_ref, wu_ref, wd_ref = refs[:4]
    proj_in = refs[4:4 + N_PROJ_OPERANDS] if with_proj else []
    refs = refs[4 + len(proj_in):]
    out_ref, a_scr = refs[0], refs[-1]
    proj_out = refs[1:-1]
    tm = x_ref.shape[0]
    halves = [slice(0, tm // 2), slice(tm // 2, tm)]
    xs, hns = [], []
    for r in halves:
        x = x_ref[r, :]
        if with_attn:
            m = lax.dot_general(o_ref[:, r], wo_ref[...], TN_DIMS, preferred_element_type=F32)
            x = x + _rms(m, g_ref[1:2, :])
        xs.append(x)
        hns.append(_rms(x, g_ref[2:3, :]).astype(BF16))
    for r, hn in zip(halves, hns):
        for f in range(0, d_ff, FF_CHUNK):
            w = min(FF_CHUNK, d_ff - f)
            gate = jnp.dot(hn, wg_ref[:, f:f + w], preferred_element_type=F32)
            up = jnp.dot(hn, wu_ref[:, f:f + w], preferred_element_type=F32)
            a_scr[r, f:f + w] = (gate * jax.nn.sigmoid(gate) * up).astype(BF16)
    outs = []
    for r, x in zip(halves, xs):
        y = jnp.dot(a_scr[r, :], wd_ref[...], preferred_element_type=F32)
        x = x + _rms(y, g_ref[3:4, :])
        out_ref[r, :] = x
        outs.append(x)
    if with_proj:
        for r, x in zip(halves, outs):
            _mla_project(x, r, *proj_in, *proj_out)


def _ffn(x, g, wg, wu, wd, attn=None, wo=None, proj_args=None):
    b, s, c = x.shape
    d_ff = wg.shape[1]
    assert d_ff % LANES == 0 and s % ROW_TILE == 0
    tm = ROW_TILE
    row = lambda i, j: (i, j, 0)
    with_attn = attn is not None
    with_proj = proj_args is not None
    in_specs = [pl.BlockSpec((None, tm, c), row)]
    args = [x]
    if with_attn:
        in_specs += [pl.BlockSpec((None, c, tm), lambda i, j: (i, 0, j)), _const_spec(wo.shape)]
        args += [attn, wo]
    in_specs += [_const_spec(g.shape), _const_spec(wg.shape), _const_spec(wu.shape),
                 _const_spec(wd.shape)]
    args += [g, wg, wu, wd]
    out_specs = [pl.BlockSpec((None, tm, c), row)]
    out_shapes = [jax.ShapeDtypeStruct(x.shape, F32)]
    if with_proj:
        assert len(proj_args) == N_PROJ_OPERANDS and tm == KV_TILE
        proj_specs, proj_out_specs, proj_out_shapes = _mla_specs(b, s, tm, proj_args)
        in_specs += proj_specs
        args += list(proj_args)
        out_specs += proj_out_specs
        out_shapes += proj_out_shapes
    kern = functools.partial(_ffn_kernel, with_attn=with_attn, with_proj=with_proj, d_ff=d_ff)
    outs = pl.pallas_call(
        kern,
        grid=(b, s // tm),
        in_specs=in_specs,
        out_specs=out_specs,
        out_shape=out_shapes,
        scratch_shapes=[pltpu.VMEM((tm, d_ff), BF16)],
        compiler_params=_params(2),
        name="ffn_attn" if with_attn else ("ffn_proj" if with_proj else "ffn"),
    )(*args)
    return outs if with_proj else outs[0]


def _mla_project(x, r, g_ref, wdn_ref, qn_ref, kvn_ref, wuqt_ref, wuk_ref, wuvt_ref,
                 cosf_ref, sinf_ref, cost_ref, sint_ref, k_out, q_out, v_out):
    tm = x.shape[0]
    xn = _rms(x, g_ref[0:1, :]).astype(BF16)
    down = jnp.dot(xn, wdn_ref[...], preferred_element_type=F32)
    c_q = down[:, :Q_LORA]
    c_kv = down[:, Q_LORA:Q_LORA + KV_LORA]
    kr = down[:, Q_LORA + KV_LORA:Q_LORA + KV_LORA + ROPE]
    kr_swapped = down[:, Q_LORA + KV_LORA + ROPE:]
    k_rope = kr * cosf_ref[r, :] + kr_swapped * sinf_ref[r, :]
    k_tail = jnp.concatenate([k_rope, jnp.zeros_like(k_rope)], axis=1).astype(BF16)
    cqn = _rms(c_q, qn_ref[...]).astype(BF16)
    ckvn = _rms(c_kv, kvn_ref[...]).astype(BF16)

    k_nope = jnp.dot(ckvn, wuk_ref[...], preferred_element_type=F32)
    for h in range(N_HEADS):
        k_out[h, r, :NOPE] = k_nope[:, h * NOPE:(h + 1) * NOPE].astype(BF16)
        k_out[h, r, NOPE:] = k_tail

    qt = lax.dot_general(wuqt_ref[...], cqn, NT_DIMS, preferred_element_type=F32) * Q_SCALE
    cos_t = cost_ref[:, r]
    sin_t = sint_ref[:, r]
    half = ROPE // 2
    hd = NOPE + ROPE
    for h in range(N_HEADS):
        base = h * hd
        q_out[h, :NOPE, r] = qt[base:base + NOPE].astype(BF16)
        x1 = qt[base + NOPE:base + NOPE + half]
        x2 = qt[base + NOPE + half:base + hd]
        q_out[h, NOPE:NOPE + half, r] = (x1 * cos_t - x2 * sin_t).astype(BF16)
        q_out[h, NOPE + half:hd, r] = (x2 * cos_t + x1 * sin_t).astype(BF16)
        q_out[h, hd:, r] = jnp.zeros((QK_PAD - hd, tm), BF16)

    vt = lax.dot_general(wuvt_ref[...], ckvn, NT_DIMS, preferred_element_type=F32)
    ones_rows = (lax.broadcasted_iota(jnp.int32, (V_ROWS - VDIM, tm), 0) == 0).astype(BF16)
    for h in range(N_HEADS):
        v_out[h, 0, :VDIM, r] = vt[h * VDIM:(h + 1) * VDIM].astype(BF16)
        v_out[h, 0, VDIM:, r] = ones_rows


def _mla_specs(b, s, tm, proj_args):
    consts = [_const_spec(a.shape) for a in proj_args[:7]]
    tables = [pl.BlockSpec((tm, ROPE), lambda i, j: (j, 0)),
              pl.BlockSpec((tm, ROPE), lambda i, j: (j, 0)),
              pl.BlockSpec((ROPE // 2, tm), lambda i, j: (0, j)),
              pl.BlockSpec((ROPE // 2, tm), lambda i, j: (0, j))]
    out_specs = [
        pl.BlockSpec((None, N_HEADS, tm, QK_PAD), lambda i, j: (i, 0, j, 0)),
        pl.BlockSpec((None, N_HEADS, QK_PAD, tm), lambda i, j: (i, 0, 0, j)),
        pl.BlockSpec((None, N_HEADS, 1, V_ROWS, tm), lambda i, j: (i, 0, j, 0, 0)),
    ]
    out_shapes = [
        jax.ShapeDtypeStruct((b, N_HEADS, s, QK_PAD), BF16),
        jax.ShapeDtypeStruct((b, N_HEADS, QK_PAD, s), BF16),
        jax.ShapeDtypeStruct((b, N_HEADS, s // tm, V_ROWS, tm), BF16),
    ]
    return consts + tables, out_specs, out_shapes


def _attn_kernel(q_ref, qnext_ref, k_ref, v_ref, o_ref, acc_ref, s_ref, cmax_ref, *, nk, tk, nb):
    qt = q_ref[...]
    tq = qt.shape[1]
    acc_ref[...] = jnp.zeros_like(acc_ref)

    def scores(qmat, i, slot):
        start = i * tk if isinstance(i, int) else pl.multiple_of(i * tk, tk)
        k = k_ref[pl.ds(start, tk), :]
        s = jnp.dot(k, qmat, preferred_element_type=F32)
        s_ref[slot] = s
        cmax_ref[slot] = jnp.max(s.reshape(tk // SUBLANES, SUBLANES, s.shape[1]), axis=0)

    def consume(i, slot, m):
        s = s_ref[slot]
        m_new = jnp.maximum(m, jnp.max(cmax_ref[slot], axis=0, keepdims=True))
        alpha = jnp.exp2(m - m_new)
        p = jnp.exp2(s - m_new).astype(BF16)
        pv = jnp.dot(v_ref[i], p, preferred_element_type=F32)
        acc_ref[...] = alpha * acc_ref[...] + pv
        return m_new

    def super_block(base, m, last):
        for t in range(2 * nb):
            if last and t >= nb:
                scores(qnext_ref[...], t - nb, t - nb)
            else:
                scores(qt, base + t + nb, (t + nb) % (2 * nb))
            m = consume(base + t, t, m)
        return m

    @pl.when(pl.program_id(2) == 0)
    def _():
        for u in range(nb):
            scores(qt, u, u)

    n_super = nk // (2 * nb)
    m = jnp.full((1, tq), -jnp.inf, F32)
    m = lax.fori_loop(
        0, n_super - 1, lambda j, c: super_block(j * (2 * nb), c, False), m)
    super_block((n_super - 1) * (2 * nb), m, True)
    out = acc_ref[:VDIM, :] / acc_ref[VDIM:VDIM + 1, :]
    o_ref[...] = out.astype(BF16)


def _attention(qt, kc, vt):
    b, h, s, _ = kc.shape
    tq, tk = Q_TILE, KV_TILE
    nk = s // tk
    nb = SCORE_LOOKAHEAD
    assert nk % (2 * nb) == 0
    kern = functools.partial(_attn_kernel, nk=nk, tk=tk, nb=nb)
    nq = s // tq
    return pl.pallas_call(
        kern,
        grid=(b, h, nq),
        in_specs=[
            pl.BlockSpec((None, None, QK_PAD, tq), lambda i, j, q: (i, j, 0, q)),
            pl.BlockSpec((None, None, QK_PAD, tq),
                         lambda i, j, q: (i, j, 0, jnp.minimum(q + 1, nq - 1))),
            pl.BlockSpec((None, None, s, QK_PAD), lambda i, j, q: (i, j, 0, 0)),
            pl.BlockSpec((None, None, nk, V_ROWS, tk), lambda i, j, q: (i, j, 0, 0, 0)),
        ],
        out_specs=pl.BlockSpec((None, VDIM, tq), lambda i, j, q: (i, j, q)),
        out_shape=jax.ShapeDtypeStruct((b, h * VDIM, s), BF16),
        scratch_shapes=[pltpu.VMEM((V_ROWS, tq), F32), pltpu.VMEM((2 * nb, tk, tq), F32),
                        pltpu.VMEM((2 * nb, SUBLANES, tq), F32)],
        compiler_params=pltpu.CompilerParams(
            dimension_semantics=("parallel", "parallel", "arbitrary"),
            vmem_limit_bytes=V7X_VMEM_LIMIT_BYTES),
        name="attention",
    )(qt, qt, kc, vt)


def _rope_tables(s):
    inv_freq = 1.0 / (ROPE_THETA ** (np.arange(0, ROPE, 2, dtype=np.float64) / ROPE))
    ang = np.arange(s, dtype=np.float64)[:, None] * inv_freq[None, :]
    cos, sin = np.cos(ang).astype(np.float32), np.sin(ang).astype(np.float32)
    cosf = np.concatenate([cos, cos], axis=1)
    sinf = np.concatenate([-sin, sin], axis=1)
    return (jnp.asarray(cosf), jnp.asarray(sinf),
            jnp.asarray(np.ascontiguousarray(cos.T)), jnp.asarray(np.ascontiguousarray(sin.T)))


def _prep_weights(fnet_w_out, mla_w_down, mla_w_uq, mla_w_ukv, mla_w_o,
                  ffn_w_gate, ffn_w_up, ffn_w_down):
    w = {}
    w["wf"] = _fold_channel_dft(fnet_w_out[0])
    wd = mla_w_down[0]
    r0 = Q_LORA + KV_LORA
    half = ROPE // 2
    w["wdn"] = jnp.concatenate([wd, wd[:, r0 + half:r0 + ROPE], wd[:, r0:r0 + half]],
                               axis=1).astype(BF16)
    w["wuqt"] = mla_w_uq[0].T.astype(BF16)
    wukv = mla_w_ukv[0].reshape(KV_LORA, N_HEADS, NOPE + VDIM)
    w["wuk"] = wukv[:, :, :NOPE].reshape(KV_LORA, N_HEADS * NOPE).astype(BF16)
    w["wuvt"] = wukv[:, :, NOPE:].reshape(KV_LORA, N_HEADS * VDIM).T.astype(BF16)
    w["wo"] = mla_w_o[0].astype(BF16)
    w["wg"] = ffn_w_gate.astype(BF16)
    w["wu"] = ffn_w_up.astype(BF16)
    w["wd"] = ffn_w_down.astype(BF16)
    return w


def _trunk(x, norm_g, fnet_b_out, mla_q_norm, mla_kv_norm, w):
    b, s, c = x.shape
    n1, n2 = _split_seq(s)
    kc = SUBLANES
    g0, g1 = norm_g[0], norm_g[1]
    tp = _fourier_a(x, g0, n1, n2, kc)
    x = _fourier_b(tp, x, w["wf"], fnet_b_out[0][None, :], g0, n1, n2, kc)
    proj_args = (g1, w["wdn"], mla_q_norm[0][None, :], mla_kv_norm[0][None, :],
                 w["wuqt"], w["wuk"], w["wuvt"], *_rope_tables(s))
    x, kc_, qt, vt = _ffn(x, g0, w["wg"][0], w["wu"][0], w["wd"][0], proj_args=proj_args)
    o = _attention(qt, kc_, vt)
    return _ffn(x, g1, w["wg"][1], w["wu"][1], w["wd"][1], attn=o, wo=w["wo"])


def kernel(x_prompt, x_sample, norm_g, fnet_w_out, fnet_b_out, mla_w_down, mla_q_norm, mla_w_uq,
           mla_kv_norm, mla_w_ukv, mla_w_o, ffn_w_gate, ffn_w_up, ffn_w_down):
    w = _prep_weights(fnet_w_out, mla_w_down, mla_w_uq, mla_w_ukv, mla_w_o,
                      ffn_w_gate, ffn_w_up, ffn_w_down)
    y_prompt = _trunk(x_prompt, norm_g, fnet_b_out, mla_q_norm, mla_kv_norm, w)
    y_sample = _trunk(x_sample, norm_g, fnet_b_out, mla_q_norm, mla_kv_norm, w)
    return (y_prompt, y_sample)
```

```python
import functools
import math

import numpy as np
import jax
import jax.numpy as jnp
from jax import lax
from jax.experimental import pallas as pl
from jax.experimental.pallas import tpu as pltpu

F32 = jnp.float32
BF16 = jnp.bfloat16

N_GROUPS = 4
N_HEADS = 8
NOPE = 128
ROPE = 64
VDIM = 128
V_ROWS = VDIM + 16
Q_LORA = 256
KV_LORA = 256
QK_PAD = 256
ROPE_THETA = 10000.0
NORM_EPS = 1e-6
Q_SCALE = math.log2(math.e) / math.sqrt(NOPE + ROPE)

V7X_VMEM_LIMIT_BYTES = 56 * 1024 * 1024
ROW_TILE = 512
Q_TILES = (1024, 512)
ATTN_VMEM_BUDGET_BYTES = 44 * 1024 * 1024
KV_TILE = 512
SCORE_LOOKAHEAD = 4
FF_CHUNK = 256
N_PROJ_OPERANDS = 11
LANES = 128
SUBLANES = 8

NT_DIMS = (((1,), (1,)), ((), ()))
TN_DIMS = (((0,), (0,)), ((), ()))


def _rms(x, g):
    ms = jnp.mean(x * x, axis=-1, keepdims=True)
    return x * lax.rsqrt(ms + NORM_EPS) * g


def _const_spec(shape):
    return pl.BlockSpec(shape, lambda *_: (0,) * len(shape), pipeline_mode=pl.Buffered(1))


def _params(n_axes):
    return pltpu.CompilerParams(dimension_semantics=("parallel",) * n_axes,
                                vmem_limit_bytes=V7X_VMEM_LIMIT_BYTES)


def _split_seq(s):
    lg = int(round(math.log2(s)))
    assert 2 ** lg == s
    n1 = 2 ** ((lg + 1) // 2)
    return n1, s // n1


def _dft_tables(n1, n2):
    n = n1 * n2
    k2 = np.arange(n2)
    a2 = 2.0 * np.pi * np.outer(k2, k2) / n2
    f2 = np.concatenate([np.cos(a2), -np.sin(a2)], axis=0)
    k1 = np.arange(n1)
    a1 = 2.0 * np.pi * np.outer(k1, k1) / n1
    c1, s1 = np.cos(a1), np.sin(a1)
    g1 = np.block([[c1, s1], [-s1, c1]])
    at = 2.0 * np.pi * np.outer(k2, np.arange(n1)) / n
    return (f2.astype(np.float32), g1.astype(np.float32),
            np.cos(at).astype(np.float32), np.sin(at).astype(np.float32))


def _flatten_slabs(lane_refs, flat_ref):
    for t, r in enumerate(lane_refs):
        flat_ref[t] = r[...].reshape(r.shape[0] * r.shape[1], LANES)


def _gather_rows(flat_ref, start, count, stride):
    return jnp.concatenate(
        [flat_ref[t, pl.ds(start, count, stride=stride), :] for t in range(flat_ref.shape[0])],
        axis=1)


def _lane_specs(rows, kc, n_lane_tiles):
    return [pl.BlockSpec((None, rows, kc, LANES),
                         functools.partial(lambda i, j, t: (i, 0, j, t), t=t))
            for t in range(n_lane_tiles)]


def _pack_pair(re, im):
    hi = lax.bitcast_convert_type(re.astype(BF16).astype(F32), jnp.uint32)
    lo = lax.bitcast_convert_type(im.astype(BF16).astype(F32), jnp.uint32)
    return hi | (lo >> 16)


def _unpack_pair(w):
    re = lax.bitcast_convert_type(w & jnp.uint32(0xFFFF0000), F32)
    im = lax.bitcast_convert_type(w << 16, F32)
    return re.astype(BF16), im.astype(BF16)


def _fourier_a_kernel(*refs, n2, kc, c):
    n_lt = c // LANES
    g_ref, f2_ref, twc_ref, tws_ref, out_ref, flat_ref = refs[n_lt:]
    _flatten_slabs(refs[:n_lt], flat_ref)
    f2 = f2_ref[...]
    g = g_ref[0:1, :]
    for jj in range(kc):
        xs = _gather_rows(flat_ref, jj, n2, kc)
        xn = _rms(xs, g).astype(BF16)
        t = jnp.dot(f2, xn, preferred_element_type=F32)
        tr, ti = t[:n2], t[n2:]
        cw = twc_ref[:, jj:jj + 1]
        sw = tws_ref[:, jj:jj + 1]
        out_ref[jj] = _pack_pair(tr * cw + ti * sw, ti * cw - tr * sw)


def _fourier_a(x, g, n1, n2, kc):
    b, s, c = x.shape
    f2, _, twc, tws = _dft_tables(n1, n2)
    f2 = jnp.asarray(f2).astype(BF16)
    twc = jnp.asarray(twc).reshape(n2, n1 // kc, kc).transpose(1, 0, 2)
    tws = jnp.asarray(tws).reshape(n2, n1 // kc, kc).transpose(1, 0, 2)
    xv = x.reshape(b, n2, n1, c)
    n_lt = c // LANES
    kern = functools.partial(_fourier_a_kernel, n2=n2, kc=kc, c=c)
    return pl.pallas_call(
        kern,
        grid=(b, n1 // kc),
        in_specs=_lane_specs(n2, kc, n_lt) + [
            _const_spec(g.shape),
            _const_spec(f2.shape),
            pl.BlockSpec((None, n2, kc), lambda i, j: (j, 0, 0)),
            pl.BlockSpec((None, n2, kc), lambda i, j: (j, 0, 0)),
        ],
        out_specs=pl.BlockSpec((None, kc, n2, c), lambda i, j: (i, j, 0, 0)),
        out_shape=jax.ShapeDtypeStruct((b, n1, n2, c), jnp.uint32),
        scratch_shapes=[pltpu.VMEM((n_lt, n2 * kc, LANES), F32)],
        compiler_params=_params(2),
        name="fourier_a",
    )(*([xv] * n_lt), g, f2, twc, tws)


def _fourier_b_kernel(*refs, n1, kc, c):
    n_lt = c // LANES
    x_ref, g1_ref, wf_ref, b_ref, g_ref, out_ref, u_scr, y_scr, flat_ref = refs[n_lt:]
    _flatten_slabs(refs[:n_lt], flat_ref)
    g1 = g1_ref[...]
    for t in range(kc):
        tr, ti = _unpack_pair(_gather_rows(flat_ref, t, n1, kc))
        u = jnp.dot(g1, jnp.concatenate([tr, ti], axis=0),
                    preferred_element_type=F32)
        u_scr[t * n1:(t + 1) * n1, :c] = u[:n1].astype(BF16)
        u_scr[t * n1:(t + 1) * n1, c:] = u[n1:].astype(BF16)
    m = jnp.dot(u_scr[...], wf_ref[...], preferred_element_type=F32) + b_ref[...]
    y = _rms(m, g_ref[1:2, :])
    pitch = y_scr.shape[1] // kc
    for lt in range(n_lt):
        for t in range(kc):
            y_scr[lt, t * pitch:t * pitch + n1, :] = y[t * n1:(t + 1) * n1,
                                                       lt * LANES:(lt + 1) * LANES]
    for k1 in range(n1):
        rows = jnp.concatenate(
            [y_scr[lt, pl.ds(k1, kc, stride=pitch), :] for lt in range(n_lt)], axis=1)
        out_ref[k1] = x_ref[k1] + rows


def _fourier_b(tp, x, wf, bias, g, n1, n2, kc):
    b, s, c = x.shape
    _, g1, _, _ = _dft_tables(n1, n2)
    g1 = jnp.asarray(g1).astype(BF16)
    xv = x.reshape(b, n1, n2, c)
    n_lt = c // LANES
    pitch = n1 + SUBLANES
    kern = functools.partial(_fourier_b_kernel, n1=n1, kc=kc, c=c)
    out = pl.pallas_call(
        kern,
        grid=(b, n2 // kc),
        in_specs=_lane_specs(n1, kc, n_lt) + [
            pl.BlockSpec((None, n1, kc, c), lambda i, j: (i, 0, j, 0)),
            _const_spec(g1.shape),
            _const_spec(wf.shape),
            _const_spec(bias.shape),
            _const_spec(g.shape),
        ],
        out_specs=pl.BlockSpec((None, n1, kc, c), lambda i, j: (i, 0, j, 0)),
        out_shape=jax.ShapeDtypeStruct(xv.shape, F32),
        scratch_shapes=[pltpu.VMEM((kc * n1, 2 * c), BF16),
                        pltpu.VMEM((n_lt, kc * pitch, LANES), F32),
                        pltpu.VMEM((n_lt, n1 * kc, LANES), jnp.uint32)],
        compiler_params=_params(2),
        name="fourier_b",
    )(*([tp] * n_lt), xv, g1, wf, bias, g)
    return out.reshape(b, s, c)


def _fold_kernel(cc_ref, sc_ref, w_ref, out_ref):
    w = w_ref[...]
    out_ref[0] = jnp.dot(cc_ref[...], w, preferred_element_type=F32,
                         precision=lax.Precision.HIGHEST).astype(BF16)
    out_ref[1] = jnp.dot(sc_ref[...], w, preferred_element_type=F32,
                         precision=lax.Precision.HIGHEST).astype(BF16)


def _fold_channel_dft(w_out):
    d = w_out.shape[0]
    gd = d // N_GROUPS
    idx = np.arange(gd)
    ang = 2.0 * np.pi * np.outer(idx, idx) / gd
    cc = jnp.asarray(np.cos(ang).astype(np.float32))
    sc = jnp.asarray(np.sin(ang).astype(np.float32))
    out = pl.pallas_call(
        _fold_kernel,
        grid=(N_GROUPS,),
        in_specs=[
            pl.BlockSpec((gd, gd), lambda i: (0, 0)),
            pl.BlockSpec((gd, gd), lambda i: (0, 0)),
            pl.BlockSpec((gd, d), lambda i: (i, 0)),
        ],
        out_specs=pl.BlockSpec((2, gd, d), lambda i: (0, i, 0)),
        out_shape=jax.ShapeDtypeStruct((2, d, d), BF16),
        compiler_params=_params(1),
        name="fold_channel_dft",
    )(cc, sc, w_out)
    return out.reshape(2 * d, d)


def _ffn_kernel(*refs, with_attn, with_proj, d_ff):
    refs = list(refs)
    x_ref = refs.pop(0)
    if with_attn:
        o_ref, wo_ref = refs.pop(0), refs.pop(0)
    g_ref, wg_ref, wu_ref, wd_ref = refs[:4]
    proj_in = refs[4:4 + N_PROJ_OPERANDS] if with_proj else []
    refs = refs[4 + len(proj_in):]
    out_ref, a_scr = refs[0], refs[-1]
    proj_out = refs[1:-1]
    tm = x_ref.shape[0]
    halves = [slice(0, tm // 2), slice(tm // 2, tm)]
    xs, hns = [], []
    for r in halves:
        x = x_ref[r, :]
        if with_attn:
            m = lax.dot_general(o_ref[:, r], wo_ref[...], TN_DIMS, preferred_element_type=F32)
            x = x + _rms(m, g_ref[1:2, :])
        xs.append(x)
        hns.append(_rms(x, g_ref[2:3, :]).astype(BF16))
    for r, hn in zip(halves, hns):
        for f in range(0, d_ff, FF_CHUNK):
            w = min(FF_CHUNK, d_ff - f)
            gate = jnp.dot(hn, wg_ref[:, f:f + w], preferred_element_type=F32)
            up = jnp.dot(hn, wu_ref[:, f:f + w], preferred_element_type=F32)
            a_scr[r, f:f + w] = (gate * jax.nn.sigmoid(gate) * up).astype(BF16)
    outs = []
    for r, x in zip(halves, xs):
        y = jnp.dot(a_scr[r, :], wd_ref[...], preferred_element_type=F32)
        x = x + _rms(y, g_ref[3:4, :])
        out_ref[r, :] = x
        outs.append(x)
    if with_proj:
        for r, x in zip(halves, outs):
            _mla_project(x, r, *proj_in, *proj_out)


def _ffn(x, g, wg, wu, wd, attn=None, wo=None, proj_args=None):
    b, s, c = x.shape
    d_ff = wg.shape[1]
    assert d_ff % LANES == 0 and s % ROW_TILE == 0
    tm = ROW_TILE
    row = lambda i, j: (i, j, 0)
    with_attn = attn is not None
    with_proj = proj_args is not None
    in_specs = [pl.BlockSpec((None, tm, c), row)]
    args = [x]
    if with_attn:
        in_specs += [pl.BlockSpec((None, c, tm), lambda i, j: (i, 0, j)), _const_spec(wo.shape)]
        args += [attn, wo]
    in_specs += [_const_spec(g.shape), _const_spec(wg.shape), _const_spec(wu.shape),
                 _const_spec(wd.shape)]
    args += [g, wg, wu, wd]
    out_specs = [pl.BlockSpec((None, tm, c), row)]
    out_shapes = [jax.ShapeDtypeStruct(x.shape, F32)]
    if with_proj:
        assert len(proj_args) == N_PROJ_OPERANDS and tm == KV_TILE
        proj_specs, proj_out_specs, proj_out_shapes = _mla_specs(b, s, tm, proj_args)
        in_specs += proj_specs
        args += list(proj_args)
        out_specs += proj_out_specs
        out_shapes += proj_out_shapes
    kern = functools.partial(_ffn_kernel, with_attn=with_attn, with_proj=with_proj, d_ff=d_ff)
    outs = pl.pallas_call(
        kern,
        grid=(b, s // tm),
        in_specs=in_specs,
        out_specs=out_specs,
        out_shape=out_shapes,
        scratch_shapes=[pltpu.VMEM((tm, d_ff), BF16)],
        compiler_params=_params(2),
        name="ffn_attn" if with_attn else ("ffn_proj" if with_proj else "ffn"),
    )(*args)
    return outs if with_proj else outs[0]


def _mla_project(x, r, g_ref, wdn_ref, qn_ref, kvn_ref, wuqt_ref, wuk_ref, wuvt_ref,
                 cosf_ref, sinf_ref, cost_ref, sint_ref, k_out, q_out, v_out):
    tm = x.shape[0]
    xn = _rms(x, g_ref[0:1, :]).astype(BF16)
    down = jnp.dot(xn, wdn_ref[...], preferred_element_type=F32)
    c_q = down[:, :Q_LORA]
    c_kv = down[:, Q_LORA:Q_LORA + KV_LORA]
    kr = down[:, Q_LORA + KV_LORA:Q_LORA + KV_LORA + ROPE]
    kr_swapped = down[:, Q_LORA + KV_LORA + ROPE:]
    k_rope = kr * cosf_ref[r, :] + kr_swapped * sinf_ref[r, :]
    k_tail = jnp.concatenate([k_rope, jnp.zeros_like(k_rope)], axis=1).astype(BF16)
    cqn = _rms(c_q, qn_ref[...]).astype(BF16)
    ckvn = _rms(c_kv, kvn_ref[...]).astype(BF16)

    k_nope = jnp.dot(ckvn, wuk_ref[...], preferred_element_type=F32)
    for h in range(N_HEADS):
        k_out[h, r, :NOPE] = k_nope[:, h * NOPE:(h + 1) * NOPE].astype(BF16)
        k_out[h, r, NOPE:] = k_tail

    qt = lax.dot_general(wuqt_ref[...], cqn, NT_DIMS, preferred_element_type=F32) * Q_SCALE
    cos_t = cost_ref[:, r]
    sin_t = sint_ref[:, r]
    half = ROPE // 2
    hd = NOPE + ROPE
    for h in range(N_HEADS):
        base = h * hd
        q_out[h, :NOPE, r] = qt[base:base + NOPE].astype(BF16)
        x1 = qt[base + NOPE:base + NOPE + half]
        x2 = qt[base + NOPE + half:base + hd]
        q_out[h, NOPE:NOPE + half, r] = (x1 * cos_t - x2 * sin_t).astype(BF16)
        q_out[h, NOPE + half:hd, r] = (x2 * cos_t + x1 * sin_t).astype(BF16)
        q_out[h, hd:, r] = jnp.zeros((QK_PAD - hd, tm), BF16)

    vt = lax.dot_general(wuvt_ref[...], ckvn, NT_DIMS, preferred_element_type=F32)
    ones_rows = (lax.broadcasted_iota(jnp.int32, (V_ROWS - VDIM, tm), 0) == 0).astype(BF16)
    for h in range(N_HEADS):
        v_out[h, 0, :VDIM, r] = vt[h * VDIM:(h + 1) * VDIM].astype(BF16)
        v_out[h, 0, VDIM:, r] = ones_rows


def _mla_specs(b, s, tm, proj_args):
    consts = [_const_spec(a.shape) for a in proj_args[:7]]
    tables = [pl.BlockSpec((tm, ROPE), lambda i, j: (j, 0)),
              pl.BlockSpec((tm, ROPE), lambda i, j: (j, 0)),
              pl.BlockSpec((ROPE // 2, tm), lambda i, j: (0, j)),
              pl.BlockSpec((ROPE // 2, tm), lambda i, j: (0, j))]
    out_specs = [
        pl.BlockSpec((None, N_HEADS, tm, QK_PAD), lambda i, j: (i, 0, j, 0)),
        pl.BlockSpec((None, N_HEADS, QK_PAD, tm), lambda i, j: (i, 0, 0, j)),
        pl.BlockSpec((None, N_HEADS, 1, V_ROWS, tm), lambda i, j: (i, 0, j, 0, 0)),
    ]
    out_shapes = [
        jax.ShapeDtypeStruct((b, N_HEADS, s, QK_PAD), BF16),
        jax.ShapeDtypeStruct((b, N_HEADS, QK_PAD, s), BF16),
        jax.ShapeDtypeStruct((b, N_HEADS, s // tm, V_ROWS, tm), BF16),
    ]
    return consts + tables, out_specs, out_shapes


def _attn_kernel(q_ref, qnext_ref, k_ref, knext_ref, v_ref, o_ref, acc_ref, s_ref, cmax_ref,
                 *, nk, tk, nb):
    qt = q_ref[...]
    tq = qt.shape[1]
    acc_ref[...] = jnp.zeros_like(acc_ref)

    def scores(qmat, i, slot, keys_ref=k_ref):
        start = i * tk if isinstance(i, int) else pl.multiple_of(i * tk, tk)
        k = keys_ref[pl.ds(start, tk), :]
        s = jnp.dot(k, qmat, preferred_element_type=F32)
        s_ref[slot] = s
        cmax_ref[slot] = jnp.max(s.reshape(tk // SUBLANES, SUBLANES, s.shape[1]), axis=0)

    def consume(i, slot, m):
        s = s_ref[slot]
        m_new = jnp.maximum(m, jnp.max(cmax_ref[slot], axis=0, keepdims=True))
        alpha = jnp.exp2(m - m_new)
        p = jnp.exp2(s - m_new).astype(BF16)
        pv = jnp.dot(v_ref[i], p, preferred_element_type=F32)
        acc_ref[...] = alpha * acc_ref[...] + pv
        return m_new

    def super_block(base, m, last):
        for t in range(2 * nb):
            if last and t >= nb:
                scores(qnext_ref[...], t - nb, t - nb, knext_ref)
            else:
                scores(qt, base + t + nb, (t + nb) % (2 * nb))
            m = consume(base + t, t, m)
        return m

    first_step = (pl.program_id(0) == 0) & (pl.program_id(1) == 0) & (pl.program_id(2) == 0)

    @pl.when(first_step)
    def _():
        for u in range(nb):
            scores(qt, u, u)

    n_super = nk // (2 * nb)
    m = jnp.full((1, tq), -jnp.inf, F32)
    for j in range(n_super):
        m = super_block(j * (2 * nb), m, j == n_super - 1)
    out = acc_ref[:VDIM, :] / acc_ref[VDIM:VDIM + 1, :]
    o_ref[...] = out.astype(BF16)


def _attention_query_tile(s, tk, nb):
    kv_bytes = 2 * 2 * s * (QK_PAD + V_ROWS)
    for tq in Q_TILES:
        slot_bytes = 2 * nb * tk * tq * 4
        if s % tq == 0 and kv_bytes + slot_bytes <= ATTN_VMEM_BUDGET_BYTES:
            return tq
    raise ValueError("no attention query tile fits VMEM")


def _attention(qt, kc, vt):
    b, h, s, _ = kc.shape
    tk = KV_TILE
    nk = s // tk
    nb = SCORE_LOOKAHEAD
    assert nk % (2 * nb) == 0
    tq = _attention_query_tile(s, tk, nb)
    kern = functools.partial(_attn_kernel, nk=nk, tk=tk, nb=nb)
    nq = s // tq

    def next_step(i, j, q):
        wrap_q = q + 1 == nq
        q1 = jnp.where(wrap_q, 0, q + 1)
        j1 = jnp.where(wrap_q, j + 1, j)
        wrap_j = j1 == h
        j1 = jnp.where(wrap_j, 0, j1)
        i1 = jnp.where(wrap_j, i + 1, i)
        end = i1 == b
        return jnp.where(end, i, i1), jnp.where(end, j, j1), jnp.where(end, q, q1)

    def qnext_map(i, j, q):
        i1, j1, q1 = next_step(i, j, q)
        return (i1, j1, 0, q1)

    def knext_map(i, j, q):
        i1, j1, _ = next_step(i, j, q)
        return (i1, j1, 0, 0)

    return pl.pallas_call(
        kern,
        grid=(b, h, nq),
        in_specs=[
            pl.BlockSpec((None, None, QK_PAD, tq), lambda i, j, q: (i, j, 0, q)),
            pl.BlockSpec((None, None, QK_PAD, tq), qnext_map),
            pl.BlockSpec((None, None, s, QK_PAD), lambda i, j, q: (i, j, 0, 0)),
            pl.BlockSpec((None, None, nb * tk, QK_PAD), knext_map),
            pl.BlockSpec((None, None, nk, V_ROWS, tk), lambda i, j, q: (i, j, 0, 0, 0)),
        ],
        out_specs=pl.BlockSpec((None, VDIM, tq), lambda i, j, q: (i, j, q)),
        out_shape=jax.ShapeDtypeStruct((b, h * VDIM, s), BF16),
        scratch_shapes=[pltpu.VMEM((V_ROWS, tq), F32), pltpu.VMEM((2 * nb, tk, tq), F32),
                        pltpu.VMEM((2 * nb, SUBLANES, tq), F32)],
        compiler_params=pltpu.CompilerParams(
            dimension_semantics=("arbitrary", "arbitrary", "arbitrary"),
            vmem_limit_bytes=V7X_VMEM_LIMIT_BYTES),
        name="attention",
    )(qt, qt, kc, kc, vt)


def _rope_tables(s):
    inv_freq = 1.0 / (ROPE_THETA ** (np.arange(0, ROPE, 2, dtype=np.float64) / ROPE))
    ang = np.arange(s, dtype=np.float64)[:, None] * inv_freq[None, :]
    cos, sin = np.cos(ang).astype(np.float32), np.sin(ang).astype(np.float32)
    cosf = np.concatenate([cos, cos], axis=1)
    sinf = np.concatenate([-sin, sin], axis=1)
    return (jnp.asarray(cosf), jnp.asarray(sinf),
            jnp.asarray(np.ascontiguousarray(cos.T)), jnp.asarray(np.ascontiguousarray(sin.T)))


def _prep_weights(fnet_w_out, mla_w_down, mla_w_uq, mla_w_ukv, mla_w_o,
                  ffn_w_gate, ffn_w_up, ffn_w_down):
    w = {}
    w["wf"] = _fold_channel_dft(fnet_w_out[0])
    wd = mla_w_down[0]
    r0 = Q_LORA + KV_LORA
    half = ROPE // 2
    w["wdn"] = jnp.concatenate([wd, wd[:, r0 + half:r0 + ROPE], wd[:, r0:r0 + half]],
                               axis=1).astype(BF16)
    w["wuqt"] = mla_w_uq[0].T.astype(BF16)
    wukv = mla_w_ukv[0].reshape(KV_LORA, N_HEADS, NOPE + VDIM)
    w["wuk"] = wukv[:, :, :NOPE].reshape(KV_LORA, N_HEADS * NOPE).astype(BF16)
    w["wuvt"] = wukv[:, :, NOPE:].reshape(KV_LORA, N_HEADS * VDIM).T.astype(BF16)
    w["wo"] = mla_w_o[0].astype(BF16)
    w["wg"] = ffn_w_gate.astype(BF16)
    w["wu"] = ffn_w_up.astype(BF16)
    w["wd"] = ffn_w_down.astype(BF16)
    return w


def _trunk(x, norm_g, fnet_b_out, mla_q_norm, mla_kv_norm, w):
    b, s, c = x.shape
    n1, n2 = _split_seq(s)
    kc = SUBLANES
    g0, g1 = norm_g[0], norm_g[1]
    tp = _fourier_a(x, g0, n1, n2, kc)
    x = _fourier_b(tp, x, w["wf"], fnet_b_out[0][None, :], g0, n1, n2, kc)
    proj_args = (g1, w["wdn"], mla_q_norm[0][None, :], mla_kv_norm[0][None, :],
                 w["wuqt"], w["wuk"], w["wuvt"], *_rope_tables(s))
    x, kc_, qt, vt = _ffn(x, g0, w["wg"][0], w["wu"][0], w["wd"][0], proj_args=proj_args)
    o = _attention(qt, kc_, vt)
    return _ffn(x, g1, w["wg"][1], w["wu"][1], w["wd"][1], attn=o, wo=w["wo"])


def kernel(x_prompt, x_sample, norm_g, fnet_w_out, fnet_b_out, mla_w_down, mla_q_norm, mla_w_uq,
           mla_kv_norm, mla_w_ukv, mla_w_o, ffn_w_gate, ffn_w_up, ffn_w_down):
    w = _prep_weights(fnet_w_out, mla_w_down, mla_w_uq, mla_w_ukv, mla_w_o,
                      ffn_w_gate, ffn_w_up, ffn_w_down)
    y_prompt = _trunk(x_prompt, norm_g, fnet_b_out, mla_q_norm, mla_kv_norm, w)
    y_sample = _trunk(x_sample, norm_g, fnet_b_out, mla_q_norm, mla_kv_norm, w)
    return (y_prompt, y_sample)
```

```python
import functools
import math

import numpy as np
import jax
import jax.numpy as jnp
from jax import lax
from jax.experimental import pallas as pl
from jax.experimental.pallas import tpu as pltpu

F32 = jnp.float32
BF16 = jnp.bfloat16

N_GROUPS = 4
N_HEADS = 8
NOPE = 128
ROPE = 64
VDIM = 128
BF16_SUBLANES = 16
V_ROWS = VDIM + BF16_SUBLANES
Q_LORA = 256
KV_LORA = 256
QK_PAD = 256
ROPE_THETA = 10000.0
NORM_EPS = 1e-6
Q_SCALE = math.log2(math.e) / math.sqrt(NOPE + ROPE)

V7X_VMEM_LIMIT_BYTES = 56 * 1024 * 1024
ROW_TILE = 512
Q_TILES = (1024, 512)
ATTN_VMEM_BUDGET_BYTES = 44 * 1024 * 1024
KV_TILE = 512
SCORE_LOOKAHEAD = 4
FF_CHUNK = 256
N_PROJ_OPERANDS = 11
LANES = 128
SUBLANES = 8

NT_DIMS = (((1,), (1,)), ((), ()))
TN_DIMS = (((0,), (0,)), ((), ()))


def _rms(x, g):
    ms = jnp.mean(x * x, axis=-1, keepdims=True)
    return x * lax.rsqrt(ms + NORM_EPS) * g


def _const_spec(shape):
    return pl.BlockSpec(shape, lambda *_: (0,) * len(shape), pipeline_mode=pl.Buffered(1))


def _params(n_axes):
    return pltpu.CompilerParams(dimension_semantics=("parallel",) * n_axes,
                                vmem_limit_bytes=V7X_VMEM_LIMIT_BYTES)


def _split_seq(s):
    lg = int(round(math.log2(s)))
    assert 2 ** lg == s
    n1 = 2 ** ((lg + 1) // 2)
    return n1, s // n1


def _second_stage_table(n1):
    k1 = np.arange(n1)
    a1 = 2.0 * np.pi * np.outer(k1, k1) / n1
    c1, s1 = np.cos(a1), np.sin(a1)
    return np.block([[c1, s1], [-s1, c1]]).astype(np.float32)


def _flatten_slabs(lane_refs, flat_ref):
    for t, r in enumerate(lane_refs):
        flat_ref[t] = r[...].reshape(r.shape[0] * r.shape[1], LANES)


def _gather_rows(flat_ref, start, count, stride):
    return jnp.concatenate(
        [flat_ref[t, pl.ds(start, count, stride=stride), :] for t in range(flat_ref.shape[0])],
        axis=1)


def _lane_specs(rows, kc, n_lane_tiles):
    return [pl.BlockSpec((None, rows, kc, LANES),
                         functools.partial(lambda i, j, t: (i, 0, j, t), t=t))
            for t in range(n_lane_tiles)]


def _pack_pair(re, im):
    hi = lax.bitcast_convert_type(re.astype(BF16).astype(F32), jnp.uint32)
    lo = lax.bitcast_convert_type(im.astype(BF16).astype(F32), jnp.uint32)
    return hi | (lo >> 16)


def _unpack_pair(w):
    re = lax.bitcast_convert_type(w & jnp.uint32(0xFFFF0000), F32)
    im = lax.bitcast_convert_type(w << 16, F32)
    return re.astype(BF16), im.astype(BF16)


def _fourier_a_kernel(*refs, n2, kc, c):
    n_lt = c // LANES
    g_ref, f2_ref, out_ref, flat_ref = refs[n_lt:]
    _flatten_slabs(refs[:n_lt], flat_ref)
    g = g_ref[0:1, :]
    for jj in range(kc):
        xs = _gather_rows(flat_ref, jj, n2, kc)
        xn = _rms(xs, g).astype(BF16)
        t = jnp.dot(f2_ref[jj].astype(BF16), xn, preferred_element_type=F32)
        out_ref[jj] = _pack_pair(t[:n2], t[n2:])


def _first_stage_tables(n1, n2):
    n = n1 * n2
    k2 = np.arange(n2, dtype=np.int64)[None, :, None]
    pos = (np.arange(n2, dtype=np.int64)[None, None, :] * n1
           + np.arange(n1, dtype=np.int64)[:, None, None])
    ang = 2.0 * np.pi * ((k2 * pos) % n).astype(np.float64) / n
    return np.concatenate([np.cos(ang), -np.sin(ang)], axis=1).astype(np.float32)


def _fourier_a(x, g, n1, n2, kc):
    b, s, c = x.shape
    f2 = jnp.asarray(_first_stage_tables(n1, n2))
    xv = x.reshape(b, n2, n1, c)
    n_lt = c // LANES
    kern = functools.partial(_fourier_a_kernel, n2=n2, kc=kc, c=c)
    return pl.pallas_call(
        kern,
        grid=(b, n1 // kc),
        in_specs=_lane_specs(n2, kc, n_lt) + [
            _const_spec(g.shape),
            pl.BlockSpec((kc, 2 * n2, n2), lambda i, j: (j, 0, 0)),
        ],
        out_specs=pl.BlockSpec((None, kc, n2, c), lambda i, j: (i, j, 0, 0)),
        out_shape=jax.ShapeDtypeStruct((b, n1, n2, c), jnp.uint32),
        scratch_shapes=[pltpu.VMEM((n_lt, n2 * kc, LANES), F32)],
        compiler_params=_params(2),
        name="fourier_a",
    )(*([xv] * n_lt), g, f2)


def _fourier_b_kernel(*refs, n1, kc, c):
    n_lt = c // LANES
    x_ref, g1_ref, wf_ref, b_ref, g_ref, out_ref, u_scr, y_scr, flat_ref = refs[n_lt:]
    _flatten_slabs(refs[:n_lt], flat_ref)
    g1 = g1_ref[...]
    for t in range(kc):
        tr, ti = _unpack_pair(_gather_rows(flat_ref, t, n1, kc))
        u = jnp.dot(g1, jnp.concatenate([tr, ti], axis=0),
                    preferred_element_type=F32)
        u_scr[t * n1:(t + 1) * n1, :c] = u[:n1].astype(BF16)
        u_scr[t * n1:(t + 1) * n1, c:] = u[n1:].astype(BF16)
    m = jnp.dot(u_scr[...], wf_ref[...], preferred_element_type=F32) + b_ref[...]
    y = _rms(m, g_ref[1:2, :])
    pitch = y_scr.shape[1] // kc
    for lt in range(n_lt):
        for t in range(kc):
            y_scr[lt, t * pitch:t * pitch + n1, :] = y[t * n1:(t + 1) * n1,
                                                       lt * LANES:(lt + 1) * LANES]
    for k1 in range(n1):
        rows = jnp.concatenate(
            [y_scr[lt, pl.ds(k1, kc, stride=pitch), :] for lt in range(n_lt)], axis=1)
        out_ref[k1] = x_ref[k1] + rows


def _fourier_b(tp, x, wf, bias, g, n1, n2, kc):
    b, s, c = x.shape
    g1 = jnp.asarray(_second_stage_table(n1)).astype(BF16)
    xv = x.reshape(b, n1, n2, c)
    n_lt = c // LANES
    pitch = n1 + SUBLANES
    kern = functools.partial(_fourier_b_kernel, n1=n1, kc=kc, c=c)
    out = pl.pallas_call(
        kern,
        grid=(b, n2 // kc),
        in_specs=_lane_specs(n1, kc, n_lt) + [
            pl.BlockSpec((None, n1, kc, c), lambda i, j: (i, 0, j, 0)),
            _const_spec(g1.shape),
            _const_spec(wf.shape),
            _const_spec(bias.shape),
            _const_spec(g.shape),
        ],
        out_specs=pl.BlockSpec((None, n1, kc, c), lambda i, j: (i, 0, j, 0)),
        out_shape=jax.ShapeDtypeStruct(xv.shape, F32),
        scratch_shapes=[pltpu.VMEM((kc * n1, 2 * c), BF16),
                        pltpu.VMEM((n_lt, kc * pitch, LANES), F32),
                        pltpu.VMEM((n_lt, n1 * kc, LANES), jnp.uint32)],
        compiler_params=_params(2),
        name="fourier_b",
    )(*([tp] * n_lt), xv, g1, wf, bias, g)
    return out.reshape(b, s, c)


def _fold_kernel(cc_ref, sc_ref, w_ref, out_ref):
    w = w_ref[...]
    out_ref[0] = jnp.dot(cc_ref[...], w, preferred_element_type=F32,
                         precision=lax.Precision.HIGHEST).astype(BF16)
    out_ref[1] = jnp.dot(sc_ref[...], w, preferred_element_type=F32,
                         precision=lax.Precision.HIGHEST).astype(BF16)


def _fold_channel_dft(w_out):
    d = w_out.shape[0]
    gd = d // N_GROUPS
    idx = np.arange(gd)
    ang = 2.0 * np.pi * np.outer(idx, idx) / gd
    cc = jnp.asarray(np.cos(ang).astype(np.float32))
    sc = jnp.asarray(np.sin(ang).astype(np.float32))
    out = pl.pallas_call(
        _fold_kernel,
        grid=(N_GROUPS,),
        in_specs=[
            pl.BlockSpec((gd, gd), lambda i: (0, 0)),
            pl.BlockSpec((gd, gd), lambda i: (0, 0)),
            pl.BlockSpec((gd, d), lambda i: (i, 0)),
        ],
        out_specs=pl.BlockSpec((2, gd, d), lambda i: (0, i, 0)),
        out_shape=jax.ShapeDtypeStruct((2, d, d), BF16),
        compiler_params=_params(1),
        name="fold_channel_dft",
    )(cc, sc, w_out)
    return out.reshape(2 * d, d)


def _ffn_kernel(*refs, with_attn, with_proj, d_ff):
    refs = list(refs)
    x_ref = refs.pop(0)
    if with_attn:
        o_ref, wo_ref = refs.pop(0), refs.pop(0)
    g_ref, wg_ref, wu_ref, wd_ref = refs[:4]
    proj_in = refs[4:4 + N_PROJ_OPERANDS] if with_proj else []
    refs = refs[4 + len(proj_in):]
    out_ref, a_scr = refs[0], refs[-1]
    proj_out = refs[1:-1]
    tm = x_ref.shape[0]
    halves = [slice(0, tm // 2), slice(tm // 2, tm)]
    xs, hns = [], []
    for r in halves:
        x = x_ref[r, :]
        if with_attn:
            m = lax.dot_general(o_ref[:, r], wo_ref[...], TN_DIMS, preferred_element_type=F32)
            x = x + _rms(m, g_ref[1:2, :])
        xs.append(x)
        hns.append(_rms(x, g_ref[2:3, :]).astype(BF16))
    for r, hn in zip(halves, hns):
        for f in range(0, d_ff, FF_CHUNK):
            w = min(FF_CHUNK, d_ff - f)
            gate = jnp.dot(hn, wg_ref[:, f:f + w], preferred_element_type=F32)
            up = jnp.dot(hn, wu_ref[:, f:f + w], preferred_element_type=F32)
            a_scr[r, f:f + w] = (gate * jax.nn.sigmoid(gate) * up).astype(BF16)
    outs = []
    for r, x in zip(halves, xs):
        y = jnp.dot(a_scr[r, :], wd_ref[...], preferred_element_type=F32)
        x = x + _rms(y, g_ref[3:4, :])
        out_ref[r, :] = x
        outs.append(x)
    if with_proj:
        for r, x in zip(halves, outs):
            _mla_project(x, r, *proj_in, *proj_out)


def _ffn(x, g, wg, wu, wd, attn=None, wo=None, proj_args=None):
    b, s, c = x.shape
    d_ff = wg.shape[1]
    assert d_ff % LANES == 0 and s % ROW_TILE == 0
    tm = ROW_TILE
    row = lambda i, j: (i, j, 0)
    with_attn = attn is not None
    with_proj = proj_args is not None
    in_specs = [pl.BlockSpec((None, tm, c), row)]
    args = [x]
    if with_attn:
        in_specs += [pl.BlockSpec((None, c, tm), lambda i, j: (i, 0, j)), _const_spec(wo.shape)]
        args += [attn, wo]
    in_specs += [_const_spec(g.shape), _const_spec(wg.shape), _const_spec(wu.shape),
                 _const_spec(wd.shape)]
    args += [g, wg, wu, wd]
    out_specs = [pl.BlockSpec((None, tm, c), row)]
    out_shapes = [jax.ShapeDtypeStruct(x.shape, F32)]
    if with_proj:
        assert len(proj_args) == N_PROJ_OPERANDS and tm == KV_TILE
        proj_specs, proj_out_specs, proj_out_shapes = _mla_specs(b, s, tm, proj_args)
        in_specs += proj_specs
        args += list(proj_args)
        out_specs += proj_out_specs
        out_shapes += proj_out_shapes
    kern = functools.partial(_ffn_kernel, with_attn=with_attn, with_proj=with_proj, d_ff=d_ff)
    outs = pl.pallas_call(
        kern,
        grid=(b, s // tm),
        in_specs=in_specs,
        out_specs=out_specs,
        out_shape=out_shapes,
        scratch_shapes=[pltpu.VMEM((tm, d_ff), BF16)],
        compiler_params=_params(2),
        name="ffn_attn" if with_attn else ("ffn_proj" if with_proj else "ffn"),
    )(*args)
    return outs if with_proj else outs[0]


def _mla_project(x, r, g_ref, wdn_ref, qn_ref, kvn_ref, wuqt_ref, wuk_ref, wuvt_ref,
                 cosf_ref, sinf_ref, cost_ref, sint_ref, k_out, q_out, v_out):
    tm = x.shape[0]
    xn = _rms(x, g_ref[0:1, :]).astype(BF16)
    down = jnp.dot(xn, wdn_ref[...], preferred_element_type=F32)
    c_q = down[:, :Q_LORA]
    c_kv = down[:, Q_LORA:Q_LORA + KV_LORA]
    kr = down[:, Q_LORA + KV_LORA:Q_LORA + KV_LORA + ROPE]
    kr_swapped = down[:, Q_LORA + KV_LORA + ROPE:]
    k_rope = kr * cosf_ref[r, :] + kr_swapped * sinf_ref[r, :]
    k_tail = jnp.concatenate([k_rope, jnp.zeros_like(k_rope)], axis=1).astype(BF16)
    cqn = _rms(c_q, qn_ref[...]).astype(BF16)
    ckvn = _rms(c_kv, kvn_ref[...]).astype(BF16)

    k_nope = jnp.dot(ckvn, wuk_ref[...], preferred_element_type=F32)
    for h in range(N_HEADS):
        k_out[h, r, :NOPE] = k_nope[:, h * NOPE:(h + 1) * NOPE].astype(BF16)
        k_out[h, r, NOPE:] = k_tail

    qt = lax.dot_general(wuqt_ref[...], cqn, NT_DIMS, preferred_element_type=F32) * Q_SCALE
    cos_t = cost_ref[:, r]
    sin_t = sint_ref[:, r]
    half = ROPE // 2
    hd = NOPE + ROPE
    for h in range(N_HEADS):
        base = h * hd
        q_out[h, :NOPE, r] = qt[base:base + NOPE].astype(BF16)
        x1 = qt[base + NOPE:base + NOPE + half]
        x2 = qt[base + NOPE + half:base + hd]
        q_out[h, NOPE:NOPE + half, r] = (x1 * cos_t - x2 * sin_t).astype(BF16)
        q_out[h, NOPE + half:hd, r] = (x2 * cos_t + x1 * sin_t).astype(BF16)
        q_out[h, hd:, r] = jnp.zeros((QK_PAD - hd, tm), BF16)

    vt = lax.dot_general(wuvt_ref[...], ckvn, NT_DIMS, preferred_element_type=F32)
    ones_rows = (lax.broadcasted_iota(jnp.int32, (V_ROWS - VDIM, tm), 0) == 0).astype(BF16)
    for h in range(N_HEADS):
        v_out[h, 0, :VDIM, r] = vt[h * VDIM:(h + 1) * VDIM].astype(BF16)
        v_out[h, 0, VDIM:, r] = ones_rows


def _mla_specs(b, s, tm, proj_args):
    consts = [_const_spec(a.shape) for a in proj_args[:7]]
    tables = [pl.BlockSpec((tm, ROPE), lambda i, j: (j, 0)),
              pl.BlockSpec((tm, ROPE), lambda i, j: (j, 0)),
              pl.BlockSpec((ROPE // 2, tm), lambda i, j: (0, j)),
              pl.BlockSpec((ROPE // 2, tm), lambda i, j: (0, j))]
    out_specs = [
        pl.BlockSpec((None, N_HEADS, tm, QK_PAD), lambda i, j: (i, 0, j, 0)),
        pl.BlockSpec((None, N_HEADS, QK_PAD, tm), lambda i, j: (i, 0, 0, j)),
        pl.BlockSpec((None, N_HEADS, 1, V_ROWS, tm), lambda i, j: (i, 0, j, 0, 0)),
    ]
    out_shapes = [
        jax.ShapeDtypeStruct((b, N_HEADS, s, QK_PAD), BF16),
        jax.ShapeDtypeStruct((b, N_HEADS, QK_PAD, s), BF16),
        jax.ShapeDtypeStruct((b, N_HEADS, s // tm, V_ROWS, tm), BF16),
    ]
    return consts + tables, out_specs, out_shapes


def _attn_kernel(q_ref, qnext_ref, k_ref, knext_ref, v_ref, o_ref, acc_ref, s_ref, cmax_ref,
                 *, nk, tk, nb):
    qt = q_ref[...]
    tq = qt.shape[1]
    acc_ref[...] = jnp.zeros_like(acc_ref)

    def scores(qmat, i, slot, keys_ref=k_ref):
        start = i * tk if isinstance(i, int) else pl.multiple_of(i * tk, tk)
        k = keys_ref[pl.ds(start, tk), :]
        s = jnp.dot(k, qmat, preferred_element_type=F32)
        s_ref[slot] = s
        cmax_ref[slot] = jnp.max(s.reshape(tk // SUBLANES, SUBLANES, s.shape[1]), axis=0)

    def consume(i, slot, m):
        s = s_ref[slot]
        m_new = jnp.maximum(m, jnp.max(cmax_ref[slot], axis=0, keepdims=True))
        alpha = jnp.exp2(m - m_new)
        p = jnp.exp2(s - m_new).astype(BF16)
        pv = jnp.dot(v_ref[i], p, preferred_element_type=F32)
        acc_ref[...] = alpha * acc_ref[...] + pv
        return m_new

    def super_block(base, m, last):
        for t in range(2 * nb):
            if last and t >= nb:
                scores(qnext_ref[...], t - nb, t - nb, knext_ref)
            else:
                scores(qt, base + t + nb, (t + nb) % (2 * nb))
            m = consume(base + t, t, m)
        return m

    first_step = (pl.program_id(0) == 0) & (pl.program_id(1) == 0) & (pl.program_id(2) == 0)

    @pl.when(first_step)
    def _():
        for u in range(nb):
            scores(qt, u, u)

    n_super = nk // (2 * nb)
    m = jnp.full((1, tq), -jnp.inf, F32)
    for j in range(n_super):
        m = super_block(j * (2 * nb), m, j == n_super - 1)
    out = acc_ref[:VDIM, :] / acc_ref[VDIM:VDIM + 1, :]
    o_ref[...] = out.astype(BF16)


def _attention_query_tile(s, tk, nb):
    kv_bytes = 2 * 2 * s * (QK_PAD + V_ROWS)
    for tq in Q_TILES:
        slot_bytes = 2 * nb * tk * tq * 4
        if s % tq == 0 and kv_bytes + slot_bytes <= ATTN_VMEM_BUDGET_BYTES:
            return tq
    raise ValueError("no attention query tile fits VMEM")


def _attention(qt, kc, vt):
    b, h, s, _ = kc.shape
    tk = KV_TILE
    nk = s // tk
    nb = SCORE_LOOKAHEAD
    assert nk % (2 * nb) == 0
    tq = _attention_query_tile(s, tk, nb)
    kern = functools.partial(_attn_kernel, nk=nk, tk=tk, nb=nb)
    nq = s // tq

    def next_step(i, j, q):
        wrap_q = q + 1 == nq
        q1 = jnp.where(wrap_q, 0, q + 1)
        j1 = jnp.where(wrap_q, j + 1, j)
        wrap_j = j1 == h
        j1 = jnp.where(wrap_j, 0, j1)
        i1 = jnp.where(wrap_j, i + 1, i)
        end = i1 == b
        return jnp.where(end, i, i1), jnp.where(end, j, j1), jnp.where(end, q, q1)

    def qnext_map(i, j, q):
        i1, j1, q1 = next_step(i, j, q)
        return (i1, j1, 0, q1)

    def knext_map(i, j, q):
        i1, j1, _ = next_step(i, j, q)
        return (i1, j1, 0, 0)

    return pl.pallas_call(
        kern,
        grid=(b, h, nq),
        in_specs=[
            pl.BlockSpec((None, None, QK_PAD, tq), lambda i, j, q: (i, j, 0, q)),
            pl.BlockSpec((None, None, QK_PAD, tq), qnext_map),
            pl.BlockSpec((None, None, s, QK_PAD), lambda i, j, q: (i, j, 0, 0)),
            pl.BlockSpec((None, None, nb * tk, QK_PAD), knext_map),
            pl.BlockSpec((None, None, nk, V_ROWS, tk), lambda i, j, q: (i, j, 0, 0, 0)),
        ],
        out_specs=pl.BlockSpec((None, VDIM, tq), lambda i, j, q: (i, j, q)),
        out_shape=jax.ShapeDtypeStruct((b, h * VDIM, s), BF16),
        scratch_shapes=[pltpu.VMEM((V_ROWS, tq), F32), pltpu.VMEM((2 * nb, tk, tq), F32),
                        pltpu.VMEM((2 * nb, SUBLANES, tq), F32)],
        compiler_params=pltpu.CompilerParams(
            dimension_semantics=("arbitrary", "arbitrary", "arbitrary"),
            vmem_limit_bytes=V7X_VMEM_LIMIT_BYTES),
        name="attention",
    )(qt, qt, kc, kc, vt)


def _rope_tables(s):
    inv_freq = 1.0 / (ROPE_THETA ** (np.arange(0, ROPE, 2, dtype=np.float64) / ROPE))
    ang = np.arange(s, dtype=np.float64)[:, None] * inv_freq[None, :]
    cos, sin = np.cos(ang).astype(np.float32), np.sin(ang).astype(np.float32)
    cosf = np.concatenate([cos, cos], axis=1)
    sinf = np.concatenate([-sin, sin], axis=1)
    return (jnp.asarray(cosf), jnp.asarray(sinf),
            jnp.asarray(np.ascontiguousarray(cos.T)), jnp.asarray(np.ascontiguousarray(sin.T)))


def _prep_weights(fnet_w_out, mla_w_down, mla_w_uq, mla_w_ukv, mla_w_o,
                  ffn_w_gate, ffn_w_up, ffn_w_down):
    w = {}
    w["wf"] = _fold_channel_dft(fnet_w_out[0])
    wd = mla_w_down[0]
    r0 = Q_LORA + KV_LORA
    half = ROPE // 2
    w["wdn"] = jnp.concatenate([wd, wd[:, r0 + half:r0 + ROPE], wd[:, r0:r0 + half]],
                               axis=1).astype(BF16)
    w["wuqt"] = mla_w_uq[0].T.astype(BF16)
    wukv = mla_w_ukv[0].reshape(KV_LORA, N_HEADS, NOPE + VDIM)
    w["wuk"] = wukv[:, :, :NOPE].reshape(KV_LORA, N_HEADS * NOPE).astype(BF16)
    w["wuvt"] = wukv[:, :, NOPE:].reshape(KV_LORA, N_HEADS * VDIM).T.astype(BF16)
    w["wo"] = mla_w_o[0].astype(BF16)
    w["wg"] = ffn_w_gate.astype(BF16)
    w["wu"] = ffn_w_up.astype(BF16)
    w["wd"] = ffn_w_down.astype(BF16)
    return w


def _trunk(x, norm_g, fnet_b_out, mla_q_norm, mla_kv_norm, w):
    b, s, c = x.shape
    n1, n2 = _split_seq(s)
    kc = SUBLANES
    g0, g1 = norm_g[0], norm_g[1]
    tp = _fourier_a(x, g0, n1, n2, kc)
    x = _fourier_b(tp, x, w["wf"], fnet_b_out[0][None, :], g0, n1, n2, kc)
    proj_args = (g1, w["wdn"], mla_q_norm[0][None, :], mla_kv_norm[0][None, :],
                 w["wuqt"], w["wuk"], w["wuvt"], *_rope_tables(s))
    x, kc_, qt, vt = _ffn(x, g0, w["wg"][0], w["wu"][0], w["wd"][0], proj_args=proj_args)
    o = _attention(qt, kc_, vt)
    return _ffn(x, g1, w["wg"][1], w["wu"][1], w["wd"][1], attn=o, wo=w["wo"])


def kernel(x_prompt, x_sample, norm_g, fnet_w_out, fnet_b_out, mla_w_down, mla_q_norm, mla_w_uq,
           mla_kv_norm, mla_w_ukv, mla_w_o, ffn_w_gate, ffn_w_up, ffn_w_down):
    w = _prep_weights(fnet_w_out, mla_w_down, mla_w_uq, mla_w_ukv, mla_w_o,
                      ffn_w_gate, ffn_w_up, ffn_w_down)
    y_prompt = _trunk(x_prompt, norm_g, fnet_b_out, mla_q_norm, mla_kv_norm, w)
    y_sample = _trunk(x_sample, norm_g, fnet_b_out, mla_q_norm, mla_kv_norm, w)
    return (y_prompt, y_sample)
```

```python
import functools
import math

import numpy as np
import jax
import jax.numpy as jnp
from jax import lax
from jax.experimental import pallas as pl
from jax.experimental.pallas import tpu as pltpu

F32 = jnp.float32
BF16 = jnp.bfloat16

N_GROUPS = 4
N_HEADS = 8
NOPE = 128
ROPE = 64
VDIM = 128
BF16_SUBLANES = 16
V_ROWS = VDIM + BF16_SUBLANES
Q_LORA = 256
KV_LORA = 256
QK_PAD = 256
ROPE_THETA = 10000.0
NORM_EPS = 1e-6
Q_SCALE = math.log2(math.e) / math.sqrt(NOPE + ROPE)

V7X_VMEM_LIMIT_BYTES = 56 * 1024 * 1024
ROW_TILE = 512
Q_TILES = (1024, 512)
ATTN_VMEM_BUDGET_BYTES = 44 * 1024 * 1024
KV_TILE = 512
SCORE_LOOKAHEAD = 4
FF_CHUNK = 256
N_PROJ_OPERANDS = 11
LANES = 128
SUBLANES = 8

NT_DIMS = (((1,), (1,)), ((), ()))
TN_DIMS = (((0,), (0,)), ((), ()))


def _rms(x, g):
    ms = jnp.mean(x * x, axis=-1, keepdims=True)
    return x * lax.rsqrt(ms + NORM_EPS) * g


def _const_spec(shape):
    return pl.BlockSpec(shape, lambda *_: (0,) * len(shape), pipeline_mode=pl.Buffered(1))


def _params(n_axes):
    return pltpu.CompilerParams(dimension_semantics=("parallel",) * n_axes,
                                vmem_limit_bytes=V7X_VMEM_LIMIT_BYTES)


def _split_seq(s):
    lg = int(round(math.log2(s)))
    assert 2 ** lg == s
    n1 = 2 ** ((lg + 1) // 2)
    return n1, s // n1


def _dft_tables(n1, n2):
    n = n1 * n2
    k2 = np.arange(n2)
    a2 = 2.0 * np.pi * np.outer(k2, k2) / n2
    f2 = np.concatenate([np.cos(a2), -np.sin(a2)], axis=0)
    k1 = np.arange(n1)
    a1 = 2.0 * np.pi * np.outer(k1, k1) / n1
    c1, s1 = np.cos(a1), np.sin(a1)
    g1 = np.block([[c1, s1], [-s1, c1]])
    at = 2.0 * np.pi * np.outer(k2, np.arange(n1)) / n
    return (f2.astype(np.float32), g1.astype(np.float32),
            np.cos(at).astype(np.float32), np.sin(at).astype(np.float32))


def _flatten_slabs(lane_refs, flat_ref):
    for t, r in enumerate(lane_refs):
        flat_ref[t] = r[...].reshape(r.shape[0] * r.shape[1], LANES)


def _gather_rows(flat_ref, start, count, stride):
    return jnp.concatenate(
        [flat_ref[t, pl.ds(start, count, stride=stride), :] for t in range(flat_ref.shape[0])],
        axis=1)


def _lane_specs(rows, kc, n_lane_tiles):
    return [pl.BlockSpec((None, rows, kc, LANES),
                         functools.partial(lambda i, j, t: (i, 0, j, t), t=t))
            for t in range(n_lane_tiles)]


def _pack_pair(re, im):
    hi = lax.bitcast_convert_type(re.astype(BF16).astype(F32), jnp.uint32)
    lo = lax.bitcast_convert_type(im.astype(BF16).astype(F32), jnp.uint32)
    return hi | (lo >> 16)


def _unpack_pair(w):
    re = lax.bitcast_convert_type(w & jnp.uint32(0xFFFF0000), F32)
    im = lax.bitcast_convert_type(w << 16, F32)
    return re.astype(BF16), im.astype(BF16)


def _fourier_a_kernel(*refs, n2, kc, c):
    n_lt = c // LANES
    g_ref, f2_ref, twc_ref, tws_ref, out_ref, flat_ref = refs[n_lt:]
    _flatten_slabs(refs[:n_lt], flat_ref)
    f2 = f2_ref[...]
    g = g_ref[0:1, :]
    for jj in range(kc):
        xs = _gather_rows(flat_ref, jj, n2, kc)
        xn = _rms(xs, g).astype(BF16)
        t = jnp.dot(f2, xn, preferred_element_type=F32)
        tr, ti = t[:n2], t[n2:]
        cw = twc_ref[:, jj:jj + 1]
        sw = tws_ref[:, jj:jj + 1]
        out_ref[jj] = _pack_pair(tr * cw + ti * sw, ti * cw - tr * sw)


def _fourier_a(x, g, n1, n2, kc):
    b, s, c = x.shape
    f2, _, twc, tws = _dft_tables(n1, n2)
    f2 = jnp.asarray(f2).astype(BF16)
    twc = jnp.asarray(twc).reshape(n2, n1 // kc, kc).transpose(1, 0, 2)
    tws = jnp.asarray(tws).reshape(n2, n1 // kc, kc).transpose(1, 0, 2)
    xv = x.reshape(b, n2, n1, c)
    n_lt = c // LANES
    kern = functools.partial(_fourier_a_kernel, n2=n2, kc=kc, c=c)
    return pl.pallas_call(
        kern,
        grid=(b, n1 // kc),
        in_specs=_lane_specs(n2, kc, n_lt) + [
            _const_spec(g.shape),
            _const_spec(f2.shape),
            pl.BlockSpec((None, n2, kc), lambda i, j: (j, 0, 0)),
            pl.BlockSpec((None, n2, kc), lambda i, j: (j, 0, 0)),
        ],
        out_specs=pl.BlockSpec((None, kc, n2, c), lambda i, j: (i, j, 0, 0)),
        out_shape=jax.ShapeDtypeStruct((b, n1, n2, c), jnp.uint32),
        scratch_shapes=[pltpu.VMEM((n_lt, n2 * kc, LANES), F32)],
        compiler_params=_params(2),
        name="fourier_a",
    )(*([xv] * n_lt), g, f2, twc, tws)


def _fourier_b_kernel(*refs, n1, kc, c):
    n_lt = c // LANES
    x_ref, g1_ref, wf_ref, b_ref, g_ref, out_ref, u_scr, y_scr, flat_ref = refs[n_lt:]
    _flatten_slabs(refs[:n_lt], flat_ref)
    g1 = g1_ref[...]
    for t in range(kc):
        tr, ti = _unpack_pair(_gather_rows(flat_ref, t, n1, kc))
        u = jnp.dot(g1, jnp.concatenate([tr, ti], axis=0),
                    preferred_element_type=F32)
        u_scr[t * n1:(t + 1) * n1, :c] = u[:n1].astype(BF16)
        u_scr[t * n1:(t + 1) * n1, c:] = u[n1:].astype(BF16)
    m = jnp.dot(u_scr[...], wf_ref[...], preferred_element_type=F32) + b_ref[...]
    y = _rms(m, g_ref[1:2, :])
    pitch = y_scr.shape[1] // kc
    for lt in range(n_lt):
        for t in range(kc):
            y_scr[lt, t * pitch:t * pitch + n1, :] = y[t * n1:(t + 1) * n1,
                                                       lt * LANES:(lt + 1) * LANES]
    for k1 in range(n1):
        rows = jnp.concatenate(
            [y_scr[lt, pl.ds(k1, kc, stride=pitch), :] for lt in range(n_lt)], axis=1)
        out_ref[k1] = x_ref[k1] + rows


def _fourier_b(tp, x, wf, bias, g, n1, n2, kc):
    b, s, c = x.shape
    _, g1, _, _ = _dft_tables(n1, n2)
    g1 = jnp.asarray(g1).astype(BF16)
    xv = x.reshape(b, n1, n2, c)
    n_lt = c // LANES
    pitch = n1 + SUBLANES
    kern = functools.partial(_fourier_b_kernel, n1=n1, kc=kc, c=c)
    out = pl.pallas_call(
        kern,
        grid=(b, n2 // kc),
        in_specs=_lane_specs(n1, kc, n_lt) + [
            pl.BlockSpec((None, n1, kc, c), lambda i, j: (i, 0, j, 0)),
            _const_spec(g1.shape),
            _const_spec(wf.shape),
            _const_spec(bias.shape),
            _const_spec(g.shape),
        ],
        out_specs=pl.BlockSpec((None, n1, kc, c), lambda i, j: (i, 0, j, 0)),
        out_shape=jax.ShapeDtypeStruct(xv.shape, F32),
        scratch_shapes=[pltpu.VMEM((kc * n1, 2 * c), BF16),
                        pltpu.VMEM((n_lt, kc * pitch, LANES), F32),
                        pltpu.VMEM((n_lt, n1 * kc, LANES), jnp.uint32)],
        compiler_params=_params(2),
        name="fourier_b",
    )(*([tp] * n_lt), xv, g1, wf, bias, g)
    return out.reshape(b, s, c)


def _fold_kernel(cc_ref, sc_ref, w_ref, out_ref):
    w = w_ref[...]
    out_ref[0] = jnp.dot(cc_ref[...], w, preferred_element_type=F32,
                         precision=lax.Precision.HIGHEST).astype(BF16)
    out_ref[1] = jnp.dot(sc_ref[...], w, preferred_element_type=F32,
                         precision=lax.Precision.HIGHEST).astype(BF16)


def _fold_channel_dft(w_out):
    d = w_out.shape[0]
    gd = d // N_GROUPS
    idx = np.arange(gd)
    ang = 2.0 * np.pi * np.outer(idx, idx) / gd
    cc = jnp.asarray(np.cos(ang).astype(np.float32))
    sc = jnp.asarray(np.sin(ang).astype(np.float32))
    out = pl.pallas_call(
        _fold_kernel,
        grid=(N_GROUPS,),
        in_specs=[
            pl.BlockSpec((gd, gd), lambda i: (0, 0)),
            pl.BlockSpec((gd, gd), lambda i: (0, 0)),
            pl.BlockSpec((gd, d), lambda i: (i, 0)),
        ],
        out_specs=pl.BlockSpec((2, gd, d), lambda i: (0, i, 0)),
        out_shape=jax.ShapeDtypeStruct((2, d, d), BF16),
        compiler_params=_params(1),
        name="fold_channel_dft",
    )(cc, sc, w_out)
    return out.reshape(2 * d, d)


def _ffn_kernel(*refs, with_attn, with_proj, d_ff):
    refs = list(refs)
    x_ref = refs.pop(0)
    if with_attn:
        o_ref, wo_ref = refs.pop(0), refs.pop(0)
    g_ref, wg_ref, wu_ref, wd_ref = refs[:4]
    proj_in = refs[4:4 + N_PROJ_OPERANDS] if with_proj else []
    refs = refs[4 + len(proj_in):]
    out_ref, a_scr = refs[0], refs[-1]
    proj_out = refs[1:-1]
    tm = x_ref.shape[0]
    halves = [slice(0, tm // 2), slice(tm // 2, tm)]
    xs, hns = [], []
    for r in halves:
        x = x_ref[r, :]
        if with_attn:
            m = lax.dot_general(o_ref[:, r], wo_ref[...], TN_DIMS, preferred_element_type=F32)
            x = x + _rms(m, g_ref[1:2, :])
        xs.append(x)
        hns.append(_rms(x, g_ref[2:3, :]).astype(BF16))
    for r, hn in zip(halves, hns):
        for f in range(0, d_ff, FF_CHUNK):
            w = min(FF_CHUNK, d_ff - f)
            gate = jnp.dot(hn, wg_ref[:, f:f + w], preferred_element_type=F32)
            up = jnp.dot(hn, wu_ref[:, f:f + w], preferred_element_type=F32)
            a_scr[r, f:f + w] = (gate * jax.nn.sigmoid(gate) * up).astype(BF16)
    outs = []
    for r, x in zip(halves, xs):
        y = jnp.dot(a_scr[r, :], wd_ref[...], preferred_element_type=F32)
        x = x + _rms(y, g_ref[3:4, :])
        out_ref[r, :] = x
        outs.append(x)
    if with_proj:
        for r, x in zip(halves, outs):
            _mla_project(x, r, *proj_in, *proj_out)


def _ffn(x, g, wg, wu, wd, attn=None, wo=None, proj_args=None):
    b, s, c = x.shape
    d_ff = wg.shape[1]
    assert d_ff % LANES == 0 and s % ROW_TILE == 0
    tm = ROW_TILE
    row = lambda i, j: (i, j, 0)
    with_attn = attn is not None
    with_proj = proj_args is not None
    in_specs = [pl.BlockSpec((None, tm, c), row)]
    args = [x]
    if with_attn:
        in_specs += [pl.BlockSpec((None, c, tm), lambda i, j: (i, 0, j)), _const_spec(wo.shape)]
        args += [attn, wo]
    in_specs += [_const_spec(g.shape), _const_spec(wg.shape), _const_spec(wu.shape),
                 _const_spec(wd.shape)]
    args += [g, wg, wu, wd]
    out_specs = [pl.BlockSpec((None, tm, c), row)]
    out_shapes = [jax.ShapeDtypeStruct(x.shape, F32)]
    if with_proj:
        assert len(proj_args) == N_PROJ_OPERANDS and tm == KV_TILE
        proj_specs, proj_out_specs, proj_out_shapes = _mla_specs(b, s, tm, proj_args)
        in_specs += proj_specs
        args += list(proj_args)
        out_specs += proj_out_specs
        out_shapes += proj_out_shapes
    kern = functools.partial(_ffn_kernel, with_attn=with_attn, with_proj=with_proj, d_ff=d_ff)
    outs = pl.pallas_call(
        kern,
        grid=(b, s // tm),
        in_specs=in_specs,
        out_specs=out_specs,
        out_shape=out_shapes,
        scratch_shapes=[pltpu.VMEM((tm, d_ff), BF16)],
        compiler_params=_params(2),
        name="ffn_attn" if with_attn else ("ffn_proj" if with_proj else "ffn"),
    )(*args)
    return outs if with_proj else outs[0]


def _mla_project(x, r, g_ref, wdn_ref, qn_ref, kvn_ref, wuqt_ref, wuk_ref, wuvt_ref,
                 cosf_ref, sinf_ref, cost_ref, sint_ref, k_out, q_out, v_out):
    tm = x.shape[0]
    xn = _rms(x, g_ref[0:1, :]).astype(BF16)
    down = jnp.dot(xn, wdn_ref[...], preferred_element_type=F32)
    c_q = down[:, :Q_LORA]
    c_kv = down[:, Q_LORA:Q_LORA + KV_LORA]
    kr = down[:, Q_LORA + KV_LORA:Q_LORA + KV_LORA + ROPE]
    kr_swapped = down[:, Q_LORA + KV_LORA + ROPE:]
    k_rope = kr * cosf_ref[r, :] + kr_swapped * sinf_ref[r, :]
    k_tail = jnp.concatenate([k_rope, jnp.zeros_like(k_rope)], axis=1).astype(BF16)
    cqn = _rms(c_q, qn_ref[...]).astype(BF16)
    ckvn = _rms(c_kv, kvn_ref[...]).astype(BF16)

    k_nope = jnp.dot(ckvn, wuk_ref[...], preferred_element_type=F32)
    for h in range(N_HEADS):
        k_out[h, r, :NOPE] = k_nope[:, h * NOPE:(h + 1) * NOPE].astype(BF16)
        k_out[h, r, NOPE:] = k_tail

    qt = lax.dot_general(wuqt_ref[...], cqn, NT_DIMS, preferred_element_type=F32) * Q_SCALE
    cos_t = cost_ref[:, r]
    sin_t = sint_ref[:, r]
    half = ROPE // 2
    hd = NOPE + ROPE
    for h in range(N_HEADS):
        base = h * hd
        q_out[h, :NOPE, r] = qt[base:base + NOPE].astype(BF16)
        x1 = qt[base + NOPE:base + NOPE + half]
        x2 = qt[base + NOPE + half:base + hd]
        q_out[h, NOPE:NOPE + half, r] = (x1 * cos_t - x2 * sin_t).astype(BF16)
        q_out[h, NOPE + half:hd, r] = (x2 * cos_t + x1 * sin_t).astype(BF16)
        q_out[h, hd:, r] = jnp.zeros((QK_PAD - hd, tm), BF16)

    vt = lax.dot_general(wuvt_ref[...], ckvn, NT_DIMS, preferred_element_type=F32)
    ones_rows = (lax.broadcasted_iota(jnp.int32, (V_ROWS - VDIM, tm), 0) == 0).astype(BF16)
    for h in range(N_HEADS):
        v_out[h, 0, :VDIM, r] = vt[h * VDIM:(h + 1) * VDIM].astype(BF16)
        v_out[h, 0, VDIM:, r] = ones_rows


def _mla_specs(b, s, tm, proj_args):
    consts = [_const_spec(a.shape) for a in proj_args[:7]]
    tables = [pl.BlockSpec((tm, ROPE), lambda i, j: (j, 0)),
              pl.BlockSpec((tm, ROPE), lambda i, j: (j, 0)),
              pl.BlockSpec((ROPE // 2, tm), lambda i, j: (0, j)),
              pl.BlockSpec((ROPE // 2, tm), lambda i, j: (0, j))]
    out_specs = [
        pl.BlockSpec((None, N_HEADS, tm, QK_PAD), lambda i, j: (i, 0, j, 0)),
        pl.BlockSpec((None, N_HEADS, QK_PAD, tm), lambda i, j: (i, 0, 0, j)),
        pl.BlockSpec((None, N_HEADS, 1, V_ROWS, tm), lambda i, j: (i, 0, j, 0, 0)),
    ]
    out_shapes = [
        jax.ShapeDtypeStruct((b, N_HEADS, s, QK_PAD), BF16),
        jax.ShapeDtypeStruct((b, N_HEADS, QK_PAD, s), BF16),
        jax.ShapeDtypeStruct((b, N_HEADS, s // tm, V_ROWS, tm), BF16),
    ]
    return consts + tables, out_specs, out_shapes


def _attn_kernel(q_ref, qnext_ref, k_ref, knext_ref, v_ref, o_ref, acc_ref, s_ref, cmax_ref,
                 *, nk, tk, nb):
    qt = q_ref[...]
    tq = qt.shape[1]
    acc_ref[...] = jnp.zeros_like(acc_ref)

    def scores(qmat, i, slot, keys_ref=k_ref):
        start = i * tk if isinstance(i, int) else pl.multiple_of(i * tk, tk)
        k = keys_ref[pl.ds(start, tk), :]
        s = jnp.dot(k, qmat, preferred_element_type=F32)
        s_ref[slot] = s
        cmax_ref[slot] = jnp.max(s.reshape(tk // SUBLANES, SUBLANES, s.shape[1]), axis=0)

    def consume(i, slot, m):
        s = s_ref[slot]
        m_new = jnp.maximum(m, jnp.max(cmax_ref[slot], axis=0, keepdims=True))
        alpha = jnp.exp2(m - m_new)
        p = jnp.exp2(s - m_new).astype(BF16)
        pv = jnp.dot(v_ref[i], p, preferred_element_type=F32)
        acc_ref[...] = alpha * acc_ref[...] + pv
        return m_new

    def super_block(base, m, last):
        for t in range(2 * nb):
            if last and t >= nb:
                scores(qnext_ref[...], t - nb, t - nb, knext_ref)
            else:
                scores(qt, base + t + nb, (t + nb) % (2 * nb))
            m = consume(base + t, t, m)
        return m

    first_step = (pl.program_id(0) == 0) & (pl.program_id(1) == 0) & (pl.program_id(2) == 0)

    @pl.when(first_step)
    def _():
        for u in range(nb):
            scores(qt, u, u)

    n_super = nk // (2 * nb)
    m = jnp.full((1, tq), -jnp.inf, F32)
    for j in range(n_super):
        m = super_block(j * (2 * nb), m, j == n_super - 1)
    out = acc_ref[:VDIM, :] / acc_ref[VDIM:VDIM + 1, :]
    o_ref[...] = out.astype(BF16)


def _attention_query_tile(s, tk, nb):
    kv_bytes = 2 * 2 * s * (QK_PAD + V_ROWS)
    for tq in Q_TILES:
        slot_bytes = 2 * nb * tk * tq * 4
        if s % tq == 0 and kv_bytes + slot_bytes <= ATTN_VMEM_BUDGET_BYTES:
            return tq
    raise ValueError("no attention query tile fits VMEM")


def _attention(qt, kc, vt):
    b, h, s, _ = kc.shape
    tk = KV_TILE
    nk = s // tk
    nb = SCORE_LOOKAHEAD
    assert nk % (2 * nb) == 0
    tq = _attention_query_tile(s, tk, nb)
    kern = functools.partial(_attn_kernel, nk=nk, tk=tk, nb=nb)
    nq = s // tq

    def next_step(i, j, q):
        wrap_q = q + 1 == nq
        q1 = jnp.where(wrap_q, 0, q + 1)
        j1 = jnp.where(wrap_q, j + 1, j)
        wrap_j = j1 == h
        j1 = jnp.where(wrap_j, 0, j1)
        i1 = jnp.where(wrap_j, i + 1, i)
        end = i1 == b
        return jnp.where(end, i, i1), jnp.where(end, j, j1), jnp.where(end, q, q1)

    def qnext_map(i, j, q):
        i1, j1, q1 = next_step(i, j, q)
        return (i1, j1, 0, q1)

    def knext_map(i, j, q):
        i1, j1, _ = next_step(i, j, q)
        return (i1, j1, 0, 0)

    return pl.pallas_call(
        kern,
        grid=(b, h, nq),
        in_specs=[
            pl.BlockSpec((None, None, QK_PAD, tq), lambda i, j, q: (i, j, 0, q)),
            pl.BlockSpec((None, None, QK_PAD, tq), qnext_map),
            pl.BlockSpec((None, None, s, QK_PAD), lambda i, j, q: (i, j, 0, 0)),
            pl.BlockSpec((None, None, nb * tk, QK_PAD), knext_map),
            pl.BlockSpec((None, None, nk, V_ROWS, tk), lambda i, j, q: (i, j, 0, 0, 0)),
        ],
        out_specs=pl.BlockSpec((None, VDIM, tq), lambda i, j, q: (i, j, q)),
        out_shape=jax.ShapeDtypeStruct((b, h * VDIM, s), BF16),
        scratch_shapes=[pltpu.VMEM((V_ROWS, tq), F32), pltpu.VMEM((2 * nb, tk, tq), F32),
                        pltpu.VMEM((2 * nb, SUBLANES, tq), F32)],
        compiler_params=pltpu.CompilerParams(
            dimension_semantics=("arbitrary", "arbitrary", "arbitrary"),
            vmem_limit_bytes=V7X_VMEM_LIMIT_BYTES),
        name="attention",
    )(qt, qt, kc, kc, vt)


def _rope_tables(s):
    inv_freq = 1.0 / (ROPE_THETA ** (np.arange(0, ROPE, 2, dtype=np.float64) / ROPE))
    ang = np.arange(s, dtype=np.float64)[:, None] * inv_freq[None, :]
    cos, sin = np.cos(ang).astype(np.float32), np.sin(ang).astype(np.float32)
    cosf = np.concatenate([cos, cos], axis=1)
    sinf = np.concatenate([-sin, sin], axis=1)
    return (jnp.asarray(cosf), jnp.asarray(sinf),
            jnp.asarray(np.ascontiguousarray(cos.T)), jnp.asarray(np.ascontiguousarray(sin.T)))


def _prep_weights(fnet_w_out, mla_w_down, mla_w_uq, mla_w_ukv, mla_w_o,
                  ffn_w_gate, ffn_w_up, ffn_w_down):
    w = {}
    w["wf"] = _fold_channel_dft(fnet_w_out[0])
    wd = mla_w_down[0]
    r0 = Q_LORA + KV_LORA
    half = ROPE // 2
    w["wdn"] = jnp.concatenate([wd, wd[:, r0 + half:r0 + ROPE], wd[:, r0:r0 + half]],
                               axis=1).astype(BF16)
    w["wuqt"] = mla_w_uq[0].T.astype(BF16)
    wukv = mla_w_ukv[0].reshape(KV_LORA, N_HEADS, NOPE + VDIM)
    w["wuk"] = wukv[:, :, :NOPE].reshape(KV_LORA, N_HEADS * NOPE).astype(BF16)
    w["wuvt"] = wukv[:, :, NOPE:].reshape(KV_LORA, N_HEADS * VDIM).T.astype(BF16)
    w["wo"] = mla_w_o[0].astype(BF16)
    w["wg"] = ffn_w_gate.astype(BF16)
    w["wu"] = ffn_w_up.astype(BF16)
    w["wd"] = ffn_w_down.astype(BF16)
    return w


def _trunk(x, norm_g, fnet_b_out, mla_q_norm, mla_kv_norm, w):
    b, s, c = x.shape
    n1, n2 = _split_seq(s)
    kc = SUBLANES
    g0, g1 = norm_g[0], norm_g[1]
    tp = _fourier_a(x, g0, n1, n2, kc)
    x = _fourier_b(tp, x, w["wf"], fnet_b_out[0][None, :], g0, n1, n2, kc)
    proj_args = (g1, w["wdn"], mla_q_norm[0][None, :], mla_kv_norm[0][None, :],
                 w["wuqt"], w["wuk"], w["wuvt"], *_rope_tables(s))
    x, kc_, qt, vt = _ffn(x, g0, w["wg"][0], w["wu"][0], w["wd"][0], proj_args=proj_args)
    o = _attention(qt, kc_, vt)
    return _ffn(x, g1, w["wg"][1], w["wu"][1], w["wd"][1], attn=o, wo=w["wo"])


def kernel(x_prompt, x_sample, norm_g, fnet_w_out, fnet_b_out, mla_w_down, mla_q_norm, mla_w_uq,
           mla_kv_norm, mla_w_ukv, mla_w_o, ffn_w_gate, ffn_w_up, ffn_w_down):
    w = _prep_weights(fnet_w_out, mla_w_down, mla_w_uq, mla_w_ukv, mla_w_o,
                      ffn_w_gate, ffn_w_up, ffn_w_down)
    y_prompt = _trunk(x_prompt, norm_g, fnet_b_out, mla_q_norm, mla_kv_norm, w)
    y_sample = _trunk(x_sample, norm_g, fnet_b_out, mla_q_norm, mla_kv_norm, w)
    return (y_prompt, y_sample)
```
